```python
import math
import jax, jax.numpy as jnp
from jax import lax
import numpy as np

D_MODEL = 1024
BATCH = 1
SEQ = 16384
DEPTH = 4

N_MIXERS = 3
ALPHA = (2.0 * DEPTH) ** 0.25
BETA = (8.0 * DEPTH) ** -0.25
LN_EPS = 1e-5
HEAD_NORM_EPS = 1e-6

HGRN_EXPAND = 128
HGRN_HEADS = D_MODEL // HGRN_EXPAND
HGRN_DK = HGRN_EXPAND
HGRN_DV = D_MODEL // HGRN_HEADS
HGRN_CHUNK = 64

MLSTM_HEADS = 8
MLSTM_DV = D_MODEL // MLSTM_HEADS
MLSTM_DQK = MLSTM_DV // 2
MLSTM_CONV = 4
MLSTM_CHUNK = 64
MLSTM_QK_W = MLSTM_HEADS * MLSTM_DQK
MLSTM_IN_W = 2 * MLSTM_QK_W + 2 * D_MODEL + 2 * MLSTM_HEADS

S5_GROUP_CH = 16
S5_GROUPS = D_MODEL // S5_GROUP_CH
S5_STATE = 64
S5_CHUNK = 1024
S5_DT_MIN = 1e-3
S5_DT_MAX = 1e-1

FFN_HIDDEN = -(-8 * D_MODEL // (3 * 256)) * 256

N_HGRN = (DEPTH + 2) // 3
N_MLSTM = (DEPTH + 1) // 3
N_S5 = DEPTH // 3

kernel_name = "hybrid_hgrn2_mlstm_s5_deepnorm"


def _layernorm(x, g, b):
    xf = x.astype(jnp.float32)
    mu = jnp.mean(xf, axis=-1, keepdims=True)
    xc = xf - mu
    var = jnp.mean(xc * xc, axis=-1, keepdims=True)
    return (xc * lax.rsqrt(var + LN_EPS) * g + b).astype(x.dtype)


def _head_rmsnorm(h, w):
    return h * lax.rsqrt(jnp.mean(h * h, axis=-1, keepdims=True) + HEAD_NORM_EPS) * w


def _to_chunks(t, n_heads, chunk):
    bsz, seq, width = t.shape
    return t.reshape(bsz, seq // chunk, chunk, n_heads, width // n_heads).transpose(1, 0, 3, 2, 4)


def _from_chunks(t):
    nc, bsz, h, c, d = t.shape
    return t.transpose(1, 0, 3, 2, 4).reshape(bsz, nc * c, h, d)


def _causal_conv(u, w):
    k_w = w.shape[0]
    seq = u.shape[1]
    up = jnp.pad(u, ((0, 0), (k_w - 1, 0), (0, 0)))
    out = up[:, 0:seq] * w[0]
    for j in range(1, k_w):
        out = out + up[:, j:j + seq] * w[j]
    return out


def _hgrn_lower_bounds(lb_logits):
    p = jax.nn.softmax(lb_logits.astype(jnp.float32), axis=0)
    c = jnp.cumsum(p, axis=0)
    return c - c[0:1]


def _hgrn2_mixer(x, w_in, norm_w, w_out, lb):
    bsz, seq, _ = x.shape
    q, f, v, g = jnp.split(x @ w_in, 4, axis=-1)
    f = f.astype(jnp.float32)
    log_f = jnp.logaddexp(jnp.log(lb), jnp.log1p(-lb) + jax.nn.log_sigmoid(f))
    k = (1.0 - lb) * jax.nn.sigmoid(-f)
    q = jax.nn.silu(q.astype(jnp.float32))
    qc = _to_chunks(q, HGRN_HEADS, HGRN_CHUNK)
    kc = _to_chunks(k, HGRN_HEADS, HGRN_CHUNK)
    vc = _to_chunks(v.astype(jnp.float32), HGRN_HEADS, HGRN_CHUNK)
    lfc = _to_chunks(log_f, HGRN_HEADS, HGRN_CHUNK)
    mask = jnp.tril(jnp.ones((HGRN_CHUNK, HGRN_CHUNK), dtype=bool))

    def step(s_prev, inp):
        qb, kb, vb, lfb = inp
        b = jnp.cumsum(lfb, axis=2)
        diff = b[:, :, :, None, :] - b[:, :, None, :, :]
        decay = jnp.exp(jnp.where(mask[:, :, None], diff, -jnp.inf))
        att = jnp.einsum('bhtd,bhsd,bhtsd->bhts', qb, kb, decay)
        o = jnp.einsum('bhts,bhse->bhte', att, vb) + jnp.einsum('bhtd,bhde->bhte', qb * jnp.exp(b), s_prev)
        b_last = b[:, :, -1]
        k_dec = kb * jnp.exp(b_last[:, :, None, :] - b)
        s_new = jnp.exp(b_last)[..., None] * s_prev + jnp.einsum('bhsd,bhse->bhde', k_dec, vb)
        return s_new, o

    s0 = jnp.zeros((bsz, HGRN_HEADS, HGRN_DK, HGRN_DV), jnp.float32)
    _, oc = lax.scan(step, s0, (qc, kc, vc, lfc))
    o = _head_rmsnorm(_from_chunks(oc), norm_w.astype(jnp.float32))
    o = o.reshape(bsz, seq, D_MODEL) * jax.nn.silu(g.astype(jnp.float32))
    return o.astype(x.dtype) @ w_out


def _mlstm_mixer(x, w_in, conv_w, gate_b, norm_w, w_out):
    bsz, seq, _ = x.shape
    proj = x @ w_in
    qk = proj[..., :2 * MLSTM_QK_W]
    v = proj[..., 2 * MLSTM_QK_W:2 * MLSTM_QK_W + D_MODEL]
    o_pre = proj[..., 2 * MLSTM_QK_W + D_MODEL:2 * MLSTM_QK_W + 2 * D_MODEL]
    gates = proj[..., 2 * MLSTM_QK_W + 2 * D_MODEL:].astype(jnp.float32) + gate_b.astype(jnp.float32)
    qk = jax.nn.silu(_causal_conv(qk, conv_w).astype(jnp.float32))
    q = qk[..., :MLSTM_QK_W]
    k = qk[..., MLSTM_QK_W:] * (MLSTM_DQK ** -0.5)
    log_i = gates[..., :MLSTM_HEADS]
    log_f = jax.nn.log_sigmoid(gates[..., MLSTM_HEADS:])
    qc = _to_chunks(q, MLSTM_HEADS, MLSTM_CHUNK)
    kc = _to_chunks(k, MLSTM_HEADS, MLSTM_CHUNK)
    vc = _to_chunks(v.astype(jnp.float32), MLSTM_HEADS, MLSTM_CHUNK)
    lic = _to_chunks(log_i, MLSTM_HEADS, MLSTM_CHUNK)[..., 0]
    lfc = _to_chunks(log_f, MLSTM_HEADS, MLSTM_CHUNK)[..., 0]
    mask = jnp.tril(jnp.ones((MLSTM_CHUNK, MLSTM_CHUNK), dtype=bool))

    def step(carry, inp):
        c_prev, n_prev, m_prev = carry
        qb, kb, vb, lib, lfb = inp
        b = jnp.cumsum(lfb, axis=-1)
        log_d = jnp.where(mask, b[..., :, None] - b[..., None, :] + lib[..., None, :], -jnp.inf)
        log_inter = b + m_prev[..., None]
        m_t = jnp.maximum(jnp.max(log_d, axis=-1), log_inter)
        d_mat = jnp.exp(log_d - m_t[..., None])
        w_inter = jnp.exp(log_inter - m_t)
        s = jnp.einsum('bhtd,bhsd->bhts', qb, kb) * d_mat
        num = jnp.einsum('bhts,bhse->bhte', s, vb) + w_inter[..., None] * jnp.einsum('bhtd,bhde->bhte', qb, c_prev)
        den = jnp.sum(s, axis=-1) + w_inter * jnp.einsum('bhtd,bhd->bht', qb, n_prev)
        h = num / jnp.maximum(jnp.abs(den), jnp.exp(-m_t))[..., None]
        b_last = b[..., -1]
        log_w = b_last[..., None] - b + lib
        m_new = jnp.maximum(b_last + m_prev, jnp.max(log_w, axis=-1))
        w_s = jnp.exp(log_w - m_new[..., None])
        w_prev = jnp.exp(b_last + m_prev - m_new)
        c_new = w_prev[..., None, None] * c_prev + jnp.einsum('bhs,bhsd,bhse->bhde', w_s, kb, vb)
        n_new = w_prev[..., None] * n_prev + jnp.einsum('bhs,bhsd->bhd', w_s, kb)
        return (c_new, n_new, m_new), h

    init = (jnp.zeros((bsz, MLSTM_HEADS, MLSTM_DQK, MLSTM_DV), jnp.float32),
            jnp.zeros((bsz, MLSTM_HEADS, MLSTM_DQK), jnp.float32),
            jnp.zeros((bsz, MLSTM_HEADS), jnp.float32))
    _, hc = lax.scan(step, init, (qc, kc, vc, lic, lfc))
    h = _head_rmsnorm(_from_chunks(hc), norm_w.astype(jnp.float32).reshape(MLSTM_HEADS, MLSTM_DV))
    h = h.reshape(bsz, seq, D_MODEL) * jax.nn.sigmoid(o_pre.astype(jnp.float32))
    return h.astype(x.dtype) @ w_out


def _cmul(ar, ai, br, bi):
    return ar * br - ai * bi, ar * bi + ai * br


def _s5_combine(e1, e2):
    a1r, a1i, b1r, b1i = e1
    a2r, a2i, b2r, b2i = e2
    ar, ai = _cmul(a2r, a2i, a1r, a1i)
    tr, ti = _cmul(a2r, a2i, b1r, b1i)
    return ar, ai, tr + b2r, ti + b2i


def _s5_mixer(x, w_in, a_re, a_im, log_dt, b_re, b_im, c_re, c_im, d_skip, w_out):
    bsz, seq, _ = x.shape
    f32 = jnp.float32
    u = (x @ w_in).astype(f32)
    a_re = a_re.astype(f32)
    a_im = a_im.astype(f32)
    b_re = b_re.astype(f32)
    b_im = b_im.astype(f32)
    c_re = c_re.astype(f32)
    c_im = c_im.astype(f32)
    dt = jnp.exp(log_dt.astype(f32))[:, None]
    mag = jnp.exp(a_re * dt)
    abar_re = mag * jnp.cos(a_im * dt)
    abar_im = mag * jnp.sin(a_im * dt)
    nr = abar_re - 1.0
    ni = abar_im
    den = a_re * a_re + a_im * a_im
    coef_re = (nr * a_re + ni * a_im) / den
    coef_im = (ni * a_re - nr * a_im) / den
    bbar_re = coef_re[..., None] * b_re - coef_im[..., None] * b_im
    bbar_im = coef_re[..., None] * b_im + coef_im[..., None] * b_re
    chunk = math.gcd(seq, S5_CHUNK)
    nc = seq // chunk
    ug = u.reshape(bsz, nc, chunk, S5_GROUPS, S5_GROUP_CH).transpose(1, 0, 2, 3, 4)
    a_r = jnp.broadcast_to(abar_re, (bsz, chunk, S5_GROUPS, S5_STATE))
    a_i = jnp.broadcast_to(abar_im, (bsz, chunk, S5_GROUPS, S5_STATE))

    def step(carry, ub):
        hr, hi = carry
        bu_re = jnp.einsum('blgc,gpc->blgp', ub, bbar_re)
        bu_im = jnp.einsum('blgc,gpc->blgp', ub, bbar_im)
        pr, pi, sr, si = lax.associative_scan(_s5_combine, (a_r, a_i, bu_re, bu_im), axis=1)
        cr, ci = _cmul(pr, pi, hr[:, None], hi[:, None])
        sr = sr + cr
        si = si + ci
        y = jnp.einsum('blgp,gcp->blgc', sr, c_re) - jnp.einsum('blgp,gcp->blgc', si, c_im)
        return (sr[:, -1], si[:, -1]), y

    h0 = (jnp.zeros((bsz, S5_GROUPS, S5_STATE), f32), jnp.zeros((bsz, S5_GROUPS, S5_STATE), f32))
    _, yc = lax.scan(step, h0, ug)
    y = yc.transpose(1, 0, 2, 3, 4).reshape(bsz, seq, D_MODEL)
    y = y + d_skip.astype(f32) * u
    y = jax.nn.gelu(y).astype(x.dtype)
    za, zb = jnp.split(y @ w_out, 2, axis=-1)
    return za * jax.nn.sigmoid(zb)


def _swiglu(x, w_gate_up, w_down):
    gate, up = jnp.split(x @ w_gate_up, 2, axis=-1)
    return (jax.nn.silu(gate) * up) @ w_down


def setup_inputs(seed: int = 0) -> dict:
    key = jax.random.key(seed)
    ks = jax.random.split(key, 24)
    D = D_MODEL
    nrm = jax.random.normal
    x = nrm(ks[0], (BATCH, SEQ, D), jnp.float32)
    hgrn_w_in = nrm(ks[1], (N_HGRN, D, 4 * D), jnp.float32) * D ** -0.5
    hgrn_norm_w = 1.0 + 0.02 * nrm(ks[2], (N_HGRN, HGRN_DV), jnp.float32)
    hgrn_w_out = nrm(ks[3], (N_HGRN, D, D), jnp.float32) * (D ** -0.5 * BETA)
    hgrn_lb_logits = 0.5 * nrm(ks[4], (DEPTH, HGRN_HEADS * HGRN_DK), jnp.float32)
    mlstm_w_in = nrm(ks[5], (N_MLSTM, D, MLSTM_IN_W), jnp.float32) * D ** -0.5
    mlstm_conv_w = nrm(ks[6], (N_MLSTM, MLSTM_CONV, 2 * MLSTM_QK_W), jnp.float32) * MLSTM_CONV ** -0.5
    i_bias = 0.1 * nrm(ks[7], (N_MLSTM, MLSTM_HEADS), jnp.float32)
    f_bias = jnp.linspace(3.0, 6.0, MLSTM_HEADS, dtype=jnp.float32)[None] + 0.1 * nrm(ks[8], (N_MLSTM, MLSTM_HEADS), jnp.float32)
    mlstm_gate_b = jnp.concatenate([i_bias, f_bias], axis=-1)
    mlstm_norm_w = 1.0 + 0.02 * nrm(ks[9], (N_MLSTM, D), jnp.float32)
    mlstm_w_out = nrm(ks[10], (N_MLSTM, D, D), jnp.float32) * (D ** -0.5 * BETA)
    s5_w_in = nrm(ks[11], (N_S5, D, D), jnp.float32) * D ** -0.5
    s5_a_re = -0.5 + 0.01 * nrm(ks[12], (N_S5, S5_GROUPS, S5_STATE), jnp.float32)
    s5_a_im = (jnp.pi * jnp.arange(S5_STATE, dtype=jnp.float32))[None, None] + 0.01 * nrm(ks[13], (N_S5, S5_GROUPS, S5_STATE), jnp.float32)
    s5_log_dt = jax.random.uniform(ks[14], (N_S5, S5_GROUPS), jnp.float32, math.log(S5_DT_MIN), math.log(S5_DT_MAX))
    s5_b_re = nrm(ks[15], (N_S5, S5_GROUPS, S5_STATE, S5_GROUP_CH), jnp.float32) * (2 * S5_GROUP_CH) ** -0.5
    s5_b_im = nrm(ks[16], (N_S5, S5_GROUPS, S5_STATE, S5_GROUP_CH), jnp.float32) * (2 * S5_GROUP_CH) ** -0.5
    s5_c_re = nrm(ks[17], (N_S5, S5_GROUPS, S5_GROUP_CH, S5_STATE), jnp.float32) * S5_STATE ** -0.5
    s5_c_im = nrm(ks[18], (N_S5, S5_GROUPS, S5_GROUP_CH, S5_STATE), jnp.float32) * S5_STATE ** -0.5
    s5_d = nrm(ks[19], (N_S5, D), jnp.float32)
    s5_w_out = nrm(ks[20], (N_S5, D, 2 * D), jnp.float32) * (D ** -0.5 * BETA)
    ffn_w_gate_up = nrm(ks[21], (DEPTH, D, 2 * FFN_HIDDEN), jnp.float32) * D ** -0.5
    ffn_w_down = nrm(ks[22], (DEPTH, FFN_HIDDEN, D), jnp.float32) * (FFN_HIDDEN ** -0.5 * BETA)
    kg, kb = jax.random.split(ks[23])
    ln_g = 1.0 + 0.02 * nrm(kg, (DEPTH, 2, D), jnp.float32)
    ln_b = 0.02 * nrm(kb, (DEPTH, 2, D), jnp.float32)
    return {"x": x, "hgrn_w_in": hgrn_w_in, "hgrn_norm_w": hgrn_norm_w, "hgrn_w_out": hgrn_w_out,
            "hgrn_lb_logits": hgrn_lb_logits, "mlstm_w_in": mlstm_w_in, "mlstm_conv_w": mlstm_conv_w,
            "mlstm_gate_b": mlstm_gate_b, "mlstm_norm_w": mlstm_norm_w, "mlstm_w_out": mlstm_w_out,
            "s5_w_in": s5_w_in, "s5_a_re": s5_a_re, "s5_a_im": s5_a_im, "s5_log_dt": s5_log_dt,
            "s5_b_re": s5_b_re, "s5_b_im": s5_b_im, "s5_c_re": s5_c_re, "s5_c_im": s5_c_im,
            "s5_d": s5_d, "s5_w_out": s5_w_out, "ffn_w_gate_up": ffn_w_gate_up, "ffn_w_down": ffn_w_down,
            "ln_g": ln_g, "ln_b": ln_b}


def reference(x, hgrn_w_in, hgrn_norm_w, hgrn_w_out, hgrn_lb_logits, mlstm_w_in, mlstm_conv_w,
              mlstm_gate_b, mlstm_norm_w, mlstm_w_out, s5_w_in, s5_a_re, s5_a_im, s5_log_dt,
              s5_b_re, s5_b_im, s5_c_re, s5_c_im, s5_d, s5_w_out, ffn_w_gate_up, ffn_w_down,
              ln_g, ln_b):
    lb_all = _hgrn_lower_bounds(hgrn_lb_logits)
    for i in range(DEPTH):
        kind = i % N_MIXERS
        j = i // N_MIXERS
        if kind == 0:
            h = _hgrn2_mixer(x, hgrn_w_in[j], hgrn_norm_w[j], hgrn_w_out[j], lb_all[i])
        elif kind == 1:
            h = _mlstm_mixer(x, mlstm_w_in[j], mlstm_conv_w[j], mlstm_gate_b[j], mlstm_norm_w[j], mlstm_w_out[j])
        else:
            h = _s5_mixer(x, s5_w_in[j], s5_a_re[j], s5_a_im[j], s5_log_dt[j], s5_b_re[j], s5_b_im[j],
                          s5_c_re[j], s5_c_im[j], s5_d[j], s5_w_out[j])
        x = _layernorm(ALPHA * x + h, ln_g[i, 0], ln_b[i, 0])
        x = _layernorm(ALPHA * x + _swiglu(x, ffn_w_gate_up[i], ffn_w_down[i]), ln_g[i, 1], ln_b[i, 1])
    return x
```

```python
import functools
import math

import jax
import jax.numpy as jnp
from jax import lax
from jax.experimental import pallas as pl
from jax.experimental.pallas import tpu as pltpu

F32 = jnp.float32
BF16 = jnp.bfloat16

D_MODEL = 1024
DEPTH = 4
N_MIXERS = 3
ALPHA = (2.0 * DEPTH) ** 0.25
LN_EPS = 1e-5
HEAD_NORM_EPS = 1e-6

HGRN_HEADS = 8
HGRN_DK = 128
HGRN_DV = 128

MLSTM_HEADS = 8
MLSTM_DV = 128
MLSTM_DQK = 64
MLSTM_CONV = 4
MLSTM_QK_W = MLSTM_HEADS * MLSTM_DQK

S5_GROUP_CH = 16
S5_GROUPS = D_MODEL // S5_GROUP_CH
S5_STATE = 64

FFN_HIDDEN = -(-8 * D_MODEL // (3 * 256)) * 256

CHUNK = 64
SUB = 8
LANES = 128
ROW_BLOCK = 256
VMEM_LIMIT_BYTES = 56 * 1024 * 1024

NEG_INF = float("-inf")


def _dot(a, b):
    return jnp.dot(a, b, preferred_element_type=F32)


def _dot_nt(a, b):
    return lax.dot_general(a, b, (((1,), (1,)), ((), ())), preferred_element_type=F32)


def _dot_tn(a, b):
    return lax.dot_general(a, b, (((0,), (0,)), ((), ())), preferred_element_type=F32)


def _layernorm(y, g, b):
    mu = jnp.mean(y, axis=-1, keepdims=True)
    yc = y - mu
    var = jnp.mean(yc * yc, axis=-1, keepdims=True)
    return yc * lax.rsqrt(var + LN_EPS) * g + b


def _log_sigmoid(z):
    return jnp.minimum(z, 0.0) - jnp.log1p(jnp.exp(-jnp.abs(z)))


def _sigmoid(z):
    return 1.0 / (1.0 + jnp.exp(-z))


def _chunk_cumsum(tri, z):
    hi = z.astype(BF16)
    r1 = z - hi.astype(F32)
    mid = r1.astype(BF16)
    lo = (r1 - mid.astype(F32)).astype(BF16)
    return _dot(tri, hi) + _dot(tri, mid) + _dot(tri, lo)


def _const_spec(shape):
    nd = len(shape)
    return pl.BlockSpec(shape, lambda i: (0,) * nd, pipeline_mode=pl.Buffered(1))


def _row_spec(tb, width):
    return pl.BlockSpec((tb, width), lambda i: (i, 0))


def _params():
    return pltpu.CompilerParams(dimension_semantics=("arbitrary",), vmem_limit_bytes=VMEM_LIMIT_BYTES)


def _ffn_kernel(x_ref, wgu_ref, wd_ref, g_ref, b_ref, o_ref):
    x = x_ref[...]
    gu = _dot(x.astype(BF16), wgu_ref[...])
    gate = gu[:, :FFN_HIDDEN]
    up = gu[:, FFN_HIDDEN:]
    act = (gate * _sigmoid(gate) * up).astype(BF16)
    y = _dot(act, wd_ref[...])
    o_ref[...] = _layernorm(ALPHA * x + y, g_ref[...], b_ref[...])


def _ffn(x, w_gate_up, w_down, g, b, tb):
    seq = x.shape[0]
    return pl.pallas_call(
        _ffn_kernel,
        out_shape=jax.ShapeDtypeStruct((seq, D_MODEL), F32),
        grid=(seq // tb,),
        in_specs=[_row_spec(tb, D_MODEL), _const_spec(w_gate_up.shape), _const_spec(w_down.shape),
                  _const_spec((1, D_MODEL)), _const_spec((1, D_MODEL))],
        out_specs=_row_spec(tb, D_MODEL),
        compiler_params=_params(),
        name="ffn",
    )(x, w_gate_up, w_down, g, b)


def _hgrn_kernel(x_ref, win_ref, lb_ref, nw_ref, wout_ref, g_ref, b_ref, tri_ref, o_ref,
                 q_s, k_s, v_s, b_s, gate_s, o_s, state_s, *, tb):
    nc = tb // CHUNK

    @pl.when(pl.program_id(0) == 0)
    def _():
        state_s[...] = jnp.zeros_like(state_s)

    x = x_ref[...]
    proj = _dot(x.astype(BF16), win_ref[...])
    q = proj[:, 0:D_MODEL]
    f = proj[:, D_MODEL:2 * D_MODEL]
    v = proj[:, 2 * D_MODEL:3 * D_MODEL]
    gt = proj[:, 3 * D_MODEL:4 * D_MODEL]
    lb = lb_ref[...]
    a1 = jnp.log(lb)
    a2 = jnp.log1p(-lb) + _log_sigmoid(f)
    log_f = jnp.maximum(a1, a2) + jnp.log1p(jnp.exp(-jnp.abs(a1 - a2)))
    k = (1.0 - lb) * _sigmoid(-f)
    q = q * _sigmoid(q)
    gate = gt * _sigmoid(gt)
    bcum = _chunk_cumsum(tri_ref[...], log_f)
    for h in range(HGRN_HEADS):
        sl = slice(h * HGRN_DK, (h + 1) * HGRN_DK)
        q_s[h] = q[:, sl]
        k_s[h] = k[:, sl]
        v_s[h] = v[:, sl]
        b_s[h] = bcum[:, sl]
        gate_s[h] = gate[:, sl]

    row = lax.broadcasted_iota(jnp.int32, (CHUNK, LANES), 0)
    rowi = lax.broadcasted_iota(jnp.int32, (CHUNK, CHUNK), 0)
    coli = lax.broadcasted_iota(jnp.int32, (CHUNK, CHUNK), 1)
    row8 = lax.broadcasted_iota(jnp.int32, (SUB, LANES), 0)
    lane8 = lax.broadcasted_iota(jnp.int32, (SUB, CHUNK), 1)
    nw = nw_ref[...]

    def chunk_body(c, st, h):
        r0 = pl.multiple_of(c * CHUNK, CHUNK)
        rows = pl.ds(r0, CHUNK)
        qc = q_s[h, rows, :]
        kc = k_s[h, rows, :]
        vc = v_s[h, rows, :]
        bc = b_s[h, rows, :]
        b_last = bc[CHUNK - 1:CHUNK, :]
        o = _dot_nt((qc * jnp.exp(bc)).astype(BF16), st.astype(BF16))
        att = jnp.zeros((CHUNK, CHUNK), F32)
        m = SUB
        while m < CHUNK:
            ref = jnp.concatenate(
                [jnp.broadcast_to(bc[blk * 2 * m + m - 1:blk * 2 * m + m, :], (2 * m, LANES))
                 for blk in range(CHUNK // (2 * m))], axis=0)
            upper = (row & m) != 0
            qh = qc * jnp.exp(jnp.where(upper, bc - ref, NEG_INF))
            kh = kc * jnp.exp(jnp.where(upper, NEG_INF, ref - bc))
            pair = _dot_nt(qh.astype(BF16), kh.astype(BF16))
            shift = int(math.log2(2 * m))
            att = att + jnp.where((rowi >> shift) == (coli >> shift), pair, 0.0)
            m *= 2
        tiles = []
        for blk in range(CHUNK // SUB):
            sl = slice(blk * SUB, (blk + 1) * SUB)
            qi, ki, bi = qc[sl], kc[sl], bc[sl]
            tile = jnp.zeros((SUB, CHUNK), F32)
            for s in range(SUB):
                decay = jnp.exp(jnp.where(row8 >= s, bi - bi[s:s + 1, :], NEG_INF))
                col = jnp.sum(qi * (ki[s:s + 1, :] * decay), axis=1, keepdims=True)
                tile = jnp.where(lane8 == blk * SUB + s, col, tile)
            tiles.append(tile)
        att = att + jnp.concatenate(tiles, axis=0)
        o = o + _dot(att.astype(BF16), vc.astype(BF16))
        k_dec = kc * jnp.exp(b_last - bc)
        st_new = st * jnp.exp(b_last) + _dot_tn(vc.astype(BF16), k_dec.astype(BF16))
        ms = jnp.mean(o * o, axis=-1, keepdims=True)
        o_s[h, rows, :] = o * lax.rsqrt(ms + HEAD_NORM_EPS) * nw * gate_s[h, rows, :]
        return st_new

    def head_body(h, carry):
        st = lax.fori_loop(0, nc, functools.partial(chunk_body, h=h), state_s[h])
        state_s[h] = st
        return carry

    lax.fori_loop(0, HGRN_HEADS, head_body, 0)

    o_all = jnp.concatenate([o_s[h] for h in range(HGRN_HEADS)], axis=1)
    y = _dot(o_all.astype(BF16), wout_ref[...])
    o_ref[...] = _layernorm(ALPHA * x + y, g_ref[...], b_ref[...])


def _chunk_tri(tb):
    i = jnp.arange(tb)
    same = (i[:, None] // CHUNK) == (i[None, :] // CHUNK)
    return (same & (i[None, :] <= i[:, None])).astype(BF16)


def _hgrn_mixer(x, w_in, norm_w, w_out, lb, g, b, tb):
    seq = x.shape[0]
    head_scratch = pltpu.VMEM((HGRN_HEADS, tb, HGRN_DK), F32)
    return pl.pallas_call(
        functools.partial(_hgrn_kernel, tb=tb),
        out_shape=jax.ShapeDtypeStruct((seq, D_MODEL), F32),
        grid=(seq // tb,),
        in_specs=[_row_spec(tb, D_MODEL), _const_spec(w_in.shape), _const_spec((1, D_MODEL)),
                  _const_spec((1, HGRN_DV)), _const_spec(w_out.shape), _const_spec((1, D_MODEL)),
                  _const_spec((1, D_MODEL)), _const_spec((tb, tb))],
        out_specs=_row_spec(tb, D_MODEL),
        scratch_shapes=[head_scratch] * 6 + [pltpu.VMEM((HGRN_HEADS, HGRN_DV, HGRN_DK), F32)],
        compiler_params=_params(),
        name="hgrn_mixer",
    )(x, w_in, lb, norm_w, w_out, g, b, _chunk_tri(tb))


def _mlstm_kernel(x_ref, win_ref, wg_ref, conv_ref, gb_ref, nw_ref, wout_ref, g_ref, b_ref, tri_ref, o_ref,
                  pad_s, q_s, k_s, v_s, gates_s, h_s, c_s, n_s, m_s, *, tb):
    nc = tb // CHUNK
    halo = SUB

    @pl.when(pl.program_id(0) == 0)
    def _():
        pad_s[0:halo, :] = jnp.zeros((halo, 2 * MLSTM_QK_W), F32)
        c_s[...] = jnp.zeros_like(c_s)
        n_s[...] = jnp.zeros_like(n_s)
        m_s[...] = jnp.zeros_like(m_s)

    x = x_ref[...]
    xb = x.astype(BF16)
    proj = _dot(xb, win_ref[...])
    v_s[...] = proj[:, 2 * MLSTM_QK_W:2 * MLSTM_QK_W + D_MODEL]
    o_pre = proj[:, 2 * MLSTM_QK_W + D_MODEL:]
    pad_s[halo:halo + tb, :] = proj[:, :2 * MLSTM_QK_W]
    conv_w = conv_ref[...]
    qk = pad_s[pl.ds(halo - MLSTM_CONV + 1, tb), :] * conv_w[0:1, :]
    for j in range(1, MLSTM_CONV):
        qk = qk + pad_s[pl.ds(halo - MLSTM_CONV + 1 + j, tb), :] * conv_w[j:j + 1, :]
    pad_s[0:halo, :] = pad_s[tb:tb + halo, :]
    qk = qk * _sigmoid(qk)
    q_s[...] = qk[:, :MLSTM_QK_W]
    k_s[...] = qk[:, MLSTM_QK_W:] * (MLSTM_DQK ** -0.5)
    gates = _dot(xb, wg_ref[...]) + gb_ref[...]
    lane = lax.broadcasted_iota(jnp.int32, (tb, LANES), 1)
    is_f = (lane >= MLSTM_HEADS) & (lane < 2 * MLSTM_HEADS)
    log_f = jnp.where(is_f, _log_sigmoid(gates), 0.0)
    gates_s[...] = jnp.where(is_f, _chunk_cumsum(tri_ref[...], log_f), gates)

    lane_c = lax.broadcasted_iota(jnp.int32, (CHUNK, LANES), 1)
    rowi = lax.broadcasted_iota(jnp.int32, (CHUNK, CHUNK), 0)
    coli = lax.broadcasted_iota(jnp.int32, (CHUNK, CHUNK), 1)
    causal = rowi >= coli
    row_state = lax.broadcasted_iota(jnp.int32, (2 * MLSTM_DQK, MLSTM_DV), 0)
    lane_n = lax.broadcasted_iota(jnp.int32, (1, LANES), 1)
    nw_all = nw_ref[...]
    sig_o = _sigmoid(o_pre)

    def pick(tile, idx):
        return jnp.sum(jnp.where(lane_c == idx, tile, 0.0), axis=1, keepdims=True)

    for p in range(MLSTM_HEADS // 2):
        lanes_p = slice(p * LANES, (p + 1) * LANES)

        def chunk_body(c, carry, p=p, lanes_p=lanes_p):
            cp, n_p, m_pair = carry
            r0 = pl.multiple_of(c * CHUNK, CHUNK)
            rows = pl.ds(r0, CHUNK)
            gc = gates_s[rows, :]
            gct = gc.T
            q_pair = q_s[rows, lanes_p]
            k_pair = k_s[rows, lanes_p]
            m_new_pair = []
            for j in range(2):
                h = 2 * p + j
                in_head = (lane_c >= j * MLSTM_DQK) & (lane_c < (j + 1) * MLSTM_DQK)
                m_prev = m_pair[j][:, 0:1]
                b_col = pick(gc, MLSTM_HEADS + h)
                li_col = pick(gc, h)
                b_row = gct[MLSTM_HEADS + h:MLSTM_HEADS + h + 1, :]
                li_row = gct[h:h + 1, :]
                log_d = jnp.where(causal, b_col - b_row + li_row, NEG_INF)
                log_inter = b_col + m_prev
                m_t = jnp.maximum(jnp.max(log_d, axis=1, keepdims=True), log_inter)
                d_mat = jnp.exp(log_d - m_t)
                w_inter = jnp.exp(log_inter - m_t)
                qm = jnp.where(in_head, q_pair, 0.0)
                km = jnp.where(in_head, k_pair, 0.0)
                qmb = qm.astype(BF16)
                vh = v_s[rows, h * MLSTM_DV:(h + 1) * MLSTM_DV]
                vhb = vh.astype(BF16)
                s = _dot_nt(qmb, km.astype(BF16)) * d_mat
                num = _dot(s.astype(BF16), vhb) + w_inter * _dot(qmb, cp.astype(BF16))
                den = jnp.sum(s, axis=1, keepdims=True) + w_inter * jnp.sum(qm * n_p, axis=1, keepdims=True)
                hh = num / jnp.maximum(jnp.abs(den), jnp.exp(-m_t))
                b_last = b_col[CHUNK - 1:CHUNK, :]
                log_w = b_last - b_col + li_col
                m_new = jnp.maximum(b_last + m_prev, jnp.max(log_w, axis=0, keepdims=True))
                w_s = jnp.exp(log_w - m_new)
                w_prev = jnp.exp(b_last + m_prev - m_new)
                kw = km * w_s
                in_rows = (row_state >= j * MLSTM_DQK) & (row_state < (j + 1) * MLSTM_DQK)
                cp = jnp.where(in_rows, w_prev * cp, cp) + _dot_tn(kw.astype(BF16), vhb)
                in_lanes = (lane_n >= j * MLSTM_DQK) & (lane_n < (j + 1) * MLSTM_DQK)
                n_p = jnp.where(in_lanes, w_prev * n_p, n_p) + jnp.sum(kw, axis=0, keepdims=True)
                m_new_pair.append(jnp.broadcast_to(m_new, (1, LANES)))
                ms = jnp.mean(hh * hh, axis=-1, keepdims=True)
                hsl = slice(h * MLSTM_DV, (h + 1) * MLSTM_DV)
                h_s[rows, hsl] = hh * lax.rsqrt(ms + HEAD_NORM_EPS) * nw_all[:, hsl]
            return cp, n_p, tuple(m_new_pair)

        init = (c_s[p], n_s[p], (m_s[2 * p], m_s[2 * p + 1]))
        cp, n_p, m_pair = lax.fori_loop(0, nc, chunk_body, init)
        c_s[p] = cp
        n_s[p] = n_p
        m_s[2 * p] = m_pair[0]
        m_s[2 * p + 1] = m_pair[1]

    y = _dot((h_s[...] * sig_o).astype(BF16), wout_ref[...])
    o_ref[...] = _layernorm(ALPHA * x + y, g_ref[...], b_ref[...])


def _mlstm_mixer(x, w_main, w_gate, conv_w, gate_b, norm_w, w_out, g, b, tb):
    seq = x.shape[0]
    wide = pltpu.VMEM((tb, D_MODEL), F32)
    return pl.pallas_call(
        functools.partial(_mlstm_kernel, tb=tb),
        out_shape=jax.ShapeDtypeStruct((seq, D_MODEL), F32),
        grid=(seq // tb,),
        in_specs=[_row_spec(tb, D_MODEL), _const_spec(w_main.shape), _const_spec(w_gate.shape),
                  _const_spec(conv_w.shape), _const_spec((1, LANES)), _const_spec((1, D_MODEL)),
                  _const_spec(w_out.shape), _const_spec((1, D_MODEL)), _const_spec((1, D_MODEL)),
                  _const_spec((tb, tb))],
        out_specs=_row_spec(tb, D_MODEL),
        scratch_shapes=[pltpu.VMEM((tb + SUB, 2 * MLSTM_QK_W), F32),
                        pltpu.VMEM((tb, MLSTM_QK_W), F32), pltpu.VMEM((tb, MLSTM_QK_W), F32),
                        wide, pltpu.VMEM((tb, LANES), F32), wide,
                        pltpu.VMEM((MLSTM_HEADS // 2, 2 * MLSTM_DQK, MLSTM_DV), F32),
                        pltpu.VMEM((MLSTM_HEADS // 2, 1, LANES), F32),
                        pltpu.VMEM((MLSTM_HEADS, 1, LANES), F32)],
        compiler_params=_params(),
        name="mlstm_mixer",
    )(x, w_main, w_gate, conv_w, gate_b, norm_w, w_out, g, b, _chunk_tri(tb))


S5_LANE_GROUPS = LANES // S5_GROUP_CH
S5_BLOCKS = D_MODEL // LANES
S5_BLOCK_STATE = S5_LANE_GROUPS * S5_STATE


def _s5_kernel(x_ref, win_ref, bre_ref, bim_ref, cre_ref, cim_ref, are_ref, aim_ref, d_ref, wout_ref,
               g_ref, b_ref, o_ref, hr_s, hi_s, y_s, *, tb):
    @pl.when(pl.program_id(0) == 0)
    def _():
        hr_s[...] = jnp.zeros_like(hr_s)
        hi_s[...] = jnp.zeros_like(hi_s)

    x = x_ref[...]
    u = _dot(x.astype(BF16), win_ref[...])
    ub = u.astype(BF16)
    row = lax.broadcasted_iota(jnp.int32, (tb, S5_BLOCK_STATE), 0)

    def shift_rows(z, k):
        if k % SUB == 0:
            return jnp.concatenate([jnp.zeros((k, z.shape[1]), F32), z[:tb - k]], axis=0)
        return jnp.where(row >= k, pltpu.roll(z, k, 0), 0.0)

    for j in range(S5_BLOCKS):
        uj = ub[:, j * LANES:(j + 1) * LANES]
        st = slice(j * S5_BLOCK_STATE, (j + 1) * S5_BLOCK_STATE)
        ar = are_ref[:, st]
        ai = aim_ref[:, st]
        hpr = hr_s[:, st]
        hpi = hi_s[:, st]
        sr = _dot(uj, bre_ref[j])
        si = _dot(uj, bim_ref[j])
        sr = sr + jnp.where(row == 0, ar * hpr - ai * hpi, 0.0)
        si = si + jnp.where(row == 0, ar * hpi + ai * hpr, 0.0)
        pr, pi = ar, ai
        k = 1
        while k < tb:
            zr = shift_rows(sr, k)
            zi = shift_rows(si, k)
            sr, si = sr + (pr * zr - pi * zi), si + (pr * zi + pi * zr)
            pr, pi = pr * pr - pi * pi, 2.0 * pr * pi
            k *= 2
        hr_s[:, st] = sr[tb - 1:tb, :]
        hi_s[:, st] = si[tb - 1:tb, :]
        y_s[:, j * LANES:(j + 1) * LANES] = _dot(sr.astype(BF16), cre_ref[j]) - _dot(si.astype(BF16), cim_ref[j])

    y = jax.nn.gelu(y_s[...] + d_ref[...] * u)
    z = _dot(y.astype(BF16), wout_ref[...])
    mix = z[:, :D_MODEL] * _sigmoid(z[:, D_MODEL:])
    o_ref[...] = _layernorm(ALPHA * x + mix, g_ref[...], b_ref[...])


def _s5_block_diag(t):
    a, b = t.shape[1], t.shape[2]
    t = t.reshape(S5_BLOCKS, S5_LANE_GROUPS, a, b)
    eye = jnp.eye(S5_LANE_GROUPS, dtype=t.dtype)
    out = jnp.einsum("jgab,gh->jgahb", t, eye)
    return out.reshape(S5_BLOCKS, S5_LANE_GROUPS * a, S5_LANE_GROUPS * b)


def _s5_mixer(x, w_in, a_re, a_im, log_dt, b_re, b_im, c_re, c_im, d_skip, w_out, g, b, tb):
    seq = x.shape[0]
    dt = jnp.exp(log_dt)[:, None]
    mag = jnp.exp(a_re * dt)
    abar_re = mag * jnp.cos(a_im * dt)
    abar_im = mag * jnp.sin(a_im * dt)
    nr = abar_re - 1.0
    ni = abar_im
    den = a_re * a_re + a_im * a_im
    coef_re = (nr * a_re + ni * a_im) / den
    coef_im = (ni * a_re - nr * a_im) / den
    bbar_re = coef_re[..., None] * b_re - coef_im[..., None] * b_im
    bbar_im = coef_re[..., None] * b_im + coef_im[..., None] * b_re
    bre = _s5_block_diag(jnp.swapaxes(bbar_re, 1, 2)).astype(BF16)
    bim = _s5_block_diag(jnp.swapaxes(bbar_im, 1, 2)).astype(BF16)
    cre = _s5_block_diag(jnp.swapaxes(c_re, 1, 2)).astype(BF16)
    cim = _s5_block_diag(jnp.swapaxes(c_im, 1, 2)).astype(BF16)
    n_state = S5_GROUPS * S5_STATE
    return pl.pallas_call(
        functools.partial(_s5_kernel, tb=tb),
        out_shape=jax.ShapeDtypeStruct((seq, D_MODEL), F32),
        grid=(seq // tb,),
        in_specs=[_row_spec(tb, D_MODEL), _const_spec(w_in.shape), _const_spec(bre.shape), _const_spec(bim.shape),
                  _const_spec(cre.shape), _const_spec(cim.shape), _const_spec((1, n_state)),
                  _const_spec((1, n_state)), _const_spec((1, D_MODEL)), _const_spec(w_out.shape),
                  _const_spec((1, D_MODEL)), _const_spec((1, D_MODEL))],
        out_specs=_row_spec(tb, D_MODEL),
        scratch_shapes=[pltpu.VMEM((1, n_state), F32), pltpu.VMEM((1, n_state), F32),
                        pltpu.VMEM((tb, D_MODEL), F32)],
        compiler_params=_params(),
        name="s5_mixer",
    )(x, w_in, bre, bim, cre, cim, abar_re.reshape(1, n_state), abar_im.reshape(1, n_state),
      d_skip.reshape(1, D_MODEL), w_out, g, b)


def _hgrn_lower_bounds(lb_logits):
    p = jax.nn.softmax(lb_logits.astype(F32), axis=0)
    c = jnp.cumsum(p, axis=0)
    return c - c[0:1]


def kernel(x, hgrn_w_in, hgrn_norm_w, hgrn_w_out, hgrn_lb_logits, mlstm_w_in, mlstm_conv_w, mlstm_gate_b, mlstm_norm_w, mlstm_w_out, s5_w_in, s5_a_re, s5_a_im, s5_log_dt, s5_b_re, s5_b_im, s5_c_re, s5_c_im, s5_d, s5_w_out, ffn_w_gate_up, ffn_w_down, ln_g, ln_b):
    bsz, seq, _ = x.shape
    tb = math.gcd(seq, ROW_BLOCK)
    assert tb % CHUNK == 0, "sequence length must be a multiple of the recurrence chunk"
    lb_all = _hgrn_lower_bounds(hgrn_lb_logits)
    row = lambda t: t.reshape(1, -1)
    outs = []
    for bi in range(bsz):
        h = x[bi]
        for i in range(DEPTH):
            kind = i % N_MIXERS
            j = i // N_MIXERS
            g0, b0 = row(ln_g[i, 0]), row(ln_b[i, 0])
            if kind == 0:
                h = _hgrn_mixer(h, hgrn_w_in[j].astype(BF16), row(hgrn_norm_w[j]), hgrn_w_out[j].astype(BF16),
                                row(lb_all[i]), g0, b0, tb)
            elif kind == 1:
                w_in = mlstm_w_in[j]
                n_main = 2 * MLSTM_QK_W + 2 * D_MODEL
                n_gate = 2 * MLSTM_HEADS
                w_gate = jnp.pad(w_in[:, n_main:], ((0, 0), (0, LANES - n_gate))).astype(BF16)
                gate_b = jnp.pad(mlstm_gate_b[j], (0, LANES - n_gate)).reshape(1, LANES)
                h = _mlstm_mixer(h, w_in[:, :n_main].astype(BF16), w_gate, mlstm_conv_w[j], gate_b,
                                 row(mlstm_norm_w[j]), mlstm_w_out[j].astype(BF16), g0, b0, tb)
            else:
                h = _s5_mixer(h, s5_w_in[j].astype(BF16), s5_a_re[j], s5_a_im[j], s5_log_dt[j], s5_b_re[j],
                              s5_b_im[j], s5_c_re[j], s5_c_im[j], s5_d[j], s5_w_out[j].astype(BF16), g0, b0, tb)
            h = _ffn(h, ffn_w_gate_up[i].astype(BF16), ffn_w_down[i].astype(BF16),
                     row(ln_g[i, 1]), row(ln_b[i, 1]), tb)
        outs.append(h)
    return jnp.stack(outs, axis=0)
```

```python
import functools
import math

import jax
import jax.numpy as jnp
from jax import lax
from jax.experimental import pallas as pl
from jax.experimental.pallas import tpu as pltpu

F32 = jnp.float32
BF16 = jnp.bfloat16

D_MODEL = 1024
DEPTH = 4
N_MIXERS = 3
ALPHA = (2.0 * DEPTH) ** 0.25
LN_EPS = 1e-5
HEAD_NORM_EPS = 1e-6

HGRN_HEADS = 8
HGRN_DK = 128
HGRN_DV = 128

MLSTM_HEADS = 8
MLSTM_DV = 128
MLSTM_DQK = 64
MLSTM_CONV = 4
MLSTM_QK_W = MLSTM_HEADS * MLSTM_DQK

S5_GROUP_CH = 16
S5_GROUPS = D_MODEL // S5_GROUP_CH
S5_STATE = 64

FFN_HIDDEN = -(-8 * D_MODEL // (3 * 256)) * 256

HGRN_HEAD_GROUP = 8
CHUNK = 64
SUB = 8
LANES = 128
ROW_BLOCK = 256
VMEM_LIMIT_BYTES = 56 * 1024 * 1024

NEG_INF = float("-inf")


def _dot(a, b):
    return jnp.dot(a, b, preferred_element_type=F32)


def _dot_nt(a, b):
    return lax.dot_general(a, b, (((1,), (1,)), ((), ())), preferred_element_type=F32)


def _dot_tn(a, b):
    return lax.dot_general(a, b, (((0,), (0,)), ((), ())), preferred_element_type=F32)


def _layernorm(y, g, b):
    mu = jnp.mean(y, axis=-1, keepdims=True)
    yc = y - mu
    var = jnp.mean(yc * yc, axis=-1, keepdims=True)
    return yc * lax.rsqrt(var + LN_EPS) * g + b


def _log_sigmoid(z):
    return jnp.minimum(z, 0.0) - jnp.log1p(jnp.exp(-jnp.abs(z)))


def _sigmoid(z):
    return 1.0 / (1.0 + jnp.exp(-z))


def _chunk_cumsum(tri, z):
    hi = z.astype(BF16)
    r1 = z - hi.astype(F32)
    mid = r1.astype(BF16)
    lo = (r1 - mid.astype(F32)).astype(BF16)
    return _dot(tri, hi) + _dot(tri, mid) + _dot(tri, lo)


def _const_spec(shape):
    nd = len(shape)
    return pl.BlockSpec(shape, lambda i: (0,) * nd, pipeline_mode=pl.Buffered(1))


def _row_spec(tb, width):
    return pl.BlockSpec((tb, width), lambda i: (i, 0))


def _params():
    return pltpu.CompilerParams(dimension_semantics=("arbitrary",), vmem_limit_bytes=VMEM_LIMIT_BYTES)


def _ffn_kernel(x_ref, wgu_ref, wd_ref, g_ref, b_ref, o_ref):
    x = x_ref[...]
    gu = _dot(x.astype(BF16), wgu_ref[...])
    gate = gu[:, :FFN_HIDDEN]
    up = gu[:, FFN_HIDDEN:]
    act = (gate * _sigmoid(gate) * up).astype(BF16)
    y = _dot(act, wd_ref[...])
    o_ref[...] = _layernorm(ALPHA * x + y, g_ref[...], b_ref[...])


def _ffn(x, w_gate_up, w_down, g, b, tb):
    seq = x.shape[0]
    return pl.pallas_call(
        _ffn_kernel,
        out_shape=jax.ShapeDtypeStruct((seq, D_MODEL), F32),
        grid=(seq // tb,),
        in_specs=[_row_spec(tb, D_MODEL), _const_spec(w_gate_up.shape), _const_spec(w_down.shape),
                  _const_spec((1, D_MODEL)), _const_spec((1, D_MODEL))],
        out_specs=_row_spec(tb, D_MODEL),
        compiler_params=_params(),
        name="ffn",
    )(x, w_gate_up, w_down, g, b)


def _hgrn_kernel(x_ref, win_ref, lb_ref, nw_ref, wout_ref, g_ref, b_ref, tri_ref, o_ref,
                 q_s, k_s, v_s, b_s, gate_s, o_s, state_s, *, tb):
    nc = tb // CHUNK

    @pl.when(pl.program_id(0) == 0)
    def _():
        state_s[...] = jnp.zeros_like(state_s)

    x = x_ref[...]
    proj = _dot(x.astype(BF16), win_ref[...])
    q = proj[:, 0:D_MODEL]
    f = proj[:, D_MODEL:2 * D_MODEL]
    v = proj[:, 2 * D_MODEL:3 * D_MODEL]
    gt = proj[:, 3 * D_MODEL:4 * D_MODEL]
    lb = lb_ref[...]
    a1 = jnp.log(lb)
    a2 = jnp.log1p(-lb) + _log_sigmoid(f)
    log_f = jnp.maximum(a1, a2) + jnp.log1p(jnp.exp(-jnp.abs(a1 - a2)))
    k = (1.0 - lb) * _sigmoid(-f)
    q = q * _sigmoid(q)
    gate = gt * _sigmoid(gt)
    bcum = _chunk_cumsum(tri_ref[...], log_f)
    for h in range(HGRN_HEADS):
        sl = slice(h * HGRN_DK, (h + 1) * HGRN_DK)
        q_s[h] = q[:, sl]
        k_s[h] = k[:, sl]
        v_s[h] = v[:, sl]
        b_s[h] = bcum[:, sl]
        gate_s[h] = gate[:, sl]

    row = lax.broadcasted_iota(jnp.int32, (CHUNK, LANES), 0)
    rowi = lax.broadcasted_iota(jnp.int32, (CHUNK, CHUNK), 0)
    coli = lax.broadcasted_iota(jnp.int32, (CHUNK, CHUNK), 1)
    row8 = lax.broadcasted_iota(jnp.int32, (SUB, LANES), 0)
    lane8 = lax.broadcasted_iota(jnp.int32, (SUB, CHUNK), 1)
    nw = nw_ref[...]

    def head_chunk(h, rows):
        st = state_s[h]
        qc = q_s[h, rows, :]
        kc = k_s[h, rows, :]
        vc = v_s[h, rows, :]
        bc = b_s[h, rows, :]
        b_last = bc[CHUNK - 1:CHUNK, :]
        o = _dot_nt((qc * jnp.exp(bc)).astype(BF16), st.astype(BF16))
        att = jnp.zeros((CHUNK, CHUNK), F32)
        m = SUB
        while m < CHUNK:
            ref = jnp.concatenate(
                [jnp.broadcast_to(bc[blk * 2 * m + m - 1:blk * 2 * m + m, :], (2 * m, LANES))
                 for blk in range(CHUNK // (2 * m))], axis=0)
            upper = (row & m) != 0
            qh = qc * jnp.exp(jnp.where(upper, bc - ref, NEG_INF))
            kh = kc * jnp.exp(jnp.where(upper, NEG_INF, ref - bc))
            pair = _dot_nt(qh.astype(BF16), kh.astype(BF16))
            shift = int(math.log2(2 * m))
            att = att + jnp.where((rowi >> shift) == (coli >> shift), pair, 0.0)
            m *= 2
        tiles = []
        for blk in range(CHUNK // SUB):
            sl = slice(blk * SUB, (blk + 1) * SUB)
            qi, ki, bi = qc[sl], kc[sl], bc[sl]
            tile = jnp.zeros((SUB, CHUNK), F32)
            for s in range(SUB):
                decay = jnp.exp(jnp.where(row8 >= s, bi - bi[s:s + 1, :], NEG_INF))
                col = jnp.sum(qi * (ki[s:s + 1, :] * decay), axis=1, keepdims=True)
                tile = jnp.where(lane8 == blk * SUB + s, col, tile)
            tiles.append(tile)
        att = att + jnp.concatenate(tiles, axis=0)
        o = o + _dot(att.astype(BF16), vc.astype(BF16))
        k_dec = kc * jnp.exp(b_last - bc)
        state_s[h] = st * jnp.exp(b_last) + _dot_tn(vc.astype(BF16), k_dec.astype(BF16))
        ms = jnp.mean(o * o, axis=-1, keepdims=True)
        o_s[h, rows, :] = o * lax.rsqrt(ms + HEAD_NORM_EPS) * nw * gate_s[h, rows, :]

    def chunk_body(c, carry, h0):
        rows = pl.ds(pl.multiple_of(c * CHUNK, CHUNK), CHUNK)
        for h in range(h0, h0 + HGRN_HEAD_GROUP):
            head_chunk(h, rows)
        return carry

    for h0 in range(0, HGRN_HEADS, HGRN_HEAD_GROUP):
        lax.fori_loop(0, nc, functools.partial(chunk_body, h0=h0), 0)

    o_all = jnp.concatenate([o_s[h] for h in range(HGRN_HEADS)], axis=1)
    y = _dot(o_all.astype(BF16), wout_ref[...])
    o_ref[...] = _layernorm(ALPHA * x + y, g_ref[...], b_ref[...])


def _chunk_tri(tb):
    i = jnp.arange(tb)
    same = (i[:, None] // CHUNK) == (i[None, :] // CHUNK)
    return (same & (i[None, :] <= i[:, None])).astype(BF16)


def _hgrn_mixer(x, w_in, norm_w, w_out, lb, g, b, tb):
    seq = x.shape[0]
    head_scratch = pltpu.VMEM((HGRN_HEADS, tb, HGRN_DK), F32)
    return pl.pallas_call(
        functools.partial(_hgrn_kernel, tb=tb),
        out_shape=jax.ShapeDtypeStruct((seq, D_MODEL), F32),
        grid=(seq // tb,),
        in_specs=[_row_spec(tb, D_MODEL), _const_spec(w_in.shape), _const_spec((1, D_MODEL)),
                  _const_spec((1, HGRN_DV)), _const_spec(w_out.shape), _const_spec((1, D_MODEL)),
                  _const_spec((1, D_MODEL)), _const_spec((tb, tb))],
        out_specs=_row_spec(tb, D_MODEL),
        scratch_shapes=[head_scratch] * 6 + [pltpu.VMEM((HGRN_HEADS, HGRN_DV, HGRN_DK), F32)],
        compiler_params=_params(),
        name="hgrn_mixer",
    )(x, w_in, lb, norm_w, w_out, g, b, _chunk_tri(tb))


def _mlstm_kernel(x_ref, win_ref, wg_ref, conv_ref, gb_ref, nw_ref, wout_ref, g_ref, b_ref, tri_ref, o_ref,
                  pad_s, q_s, k_s, v_s, gates_s, h_s, c_s, n_s, m_s, *, tb):
    nc = tb // CHUNK
    halo = SUB

    @pl.when(pl.program_id(0) == 0)
    def _():
        pad_s[0:halo, :] = jnp.zeros((halo, 2 * MLSTM_QK_W), F32)
        c_s[...] = jnp.zeros_like(c_s)
        n_s[...] = jnp.zeros_like(n_s)
        m_s[...] = jnp.zeros_like(m_s)

    x = x_ref[...]
    xb = x.astype(BF16)
    proj = _dot(xb, win_ref[...])
    v_s[...] = proj[:, 2 * MLSTM_QK_W:2 * MLSTM_QK_W + D_MODEL]
    o_pre = proj[:, 2 * MLSTM_QK_W + D_MODEL:]
    pad_s[halo:halo + tb, :] = proj[:, :2 * MLSTM_QK_W]
    conv_w = conv_ref[...]
    qk = pad_s[pl.ds(halo - MLSTM_CONV + 1, tb), :] * conv_w[0:1, :]
    for j in range(1, MLSTM_CONV):
        qk = qk + pad_s[pl.ds(halo - MLSTM_CONV + 1 + j, tb), :] * conv_w[j:j + 1, :]
    pad_s[0:halo, :] = pad_s[tb:tb + halo, :]
    qk = qk * _sigmoid(qk)
    q_s[...] = qk[:, :MLSTM_QK_W]
    k_s[...] = qk[:, MLSTM_QK_W:] * (MLSTM_DQK ** -0.5)
    gates = _dot(xb, wg_ref[...]) + gb_ref[...]
    lane = lax.broadcasted_iota(jnp.int32, (tb, LANES), 1)
    is_f = (lane >= MLSTM_HEADS) & (lane < 2 * MLSTM_HEADS)
    log_f = jnp.where(is_f, _log_sigmoid(gates), 0.0)
    gates_s[...] = jnp.where(is_f, _chunk_cumsum(tri_ref[...], log_f), gates)

    lane_c = lax.broadcasted_iota(jnp.int32, (CHUNK, LANES), 1)
    rowi = lax.broadcasted_iota(jnp.int32, (CHUNK, CHUNK), 0)
    coli = lax.broadcasted_iota(jnp.int32, (CHUNK, CHUNK), 1)
    causal = rowi >= coli
    row_state = lax.broadcasted_iota(jnp.int32, (2 * MLSTM_DQK, MLSTM_DV), 0)
    lane_n = lax.broadcasted_iota(jnp.int32, (1, LANES), 1)
    nw_all = nw_ref[...]
    sig_o = _sigmoid(o_pre)

    def pick(tile, idx):
        return jnp.sum(jnp.where(lane_c == idx, tile, 0.0), axis=1, keepdims=True)

    def chunk_body(c, carry):
        rows = pl.ds(pl.multiple_of(c * CHUNK, CHUNK), CHUNK)
        gc = gates_s[rows, :]
        gct = gc.T
        for p in range(MLSTM_HEADS // 2):
            lanes_p = slice(p * LANES, (p + 1) * LANES)
            cp = c_s[p]
            n_p = n_s[p]
            q_pair = q_s[rows, lanes_p]
            k_pair = k_s[rows, lanes_p]
            for j in range(2):
                h = 2 * p + j
                in_head = (lane_c >= j * MLSTM_DQK) & (lane_c < (j + 1) * MLSTM_DQK)
                m_prev = m_s[h][:, 0:1]
                b_col = pick(gc, MLSTM_HEADS + h)
                li_col = pick(gc, h)
                b_row = gct[MLSTM_HEADS + h:MLSTM_HEADS + h + 1, :]
                li_row = gct[h:h + 1, :]
                log_d = jnp.where(causal, b_col - b_row + li_row, NEG_INF)
                log_inter = b_col + m_prev
                m_t = jnp.maximum(jnp.max(log_d, axis=1, keepdims=True), log_inter)
                d_mat = jnp.exp(log_d - m_t)
                w_inter = jnp.exp(log_inter - m_t)
                qm = jnp.where(in_head, q_pair, 0.0)
                km = jnp.where(in_head, k_pair, 0.0)
                qmb = qm.astype(BF16)
                vh = v_s[rows, h * MLSTM_DV:(h + 1) * MLSTM_DV]
                vhb = vh.astype(BF16)
                s = _dot_nt(qmb, km.astype(BF16)) * d_mat
                num = _dot(s.astype(BF16), vhb) + w_inter * _dot(qmb, cp.astype(BF16))
                den = jnp.sum(s, axis=1, keepdims=True) + w_inter * jnp.sum(qm * n_p, axis=1, keepdims=True)
                hh = num / jnp.maximum(jnp.abs(den), jnp.exp(-m_t))
                b_last = b_col[CHUNK - 1:CHUNK, :]
                log_w = b_last - b_col + li_col
                m_new = jnp.maximum(b_last + m_prev, jnp.max(log_w, axis=0, keepdims=True))
                w_s = jnp.exp(log_w - m_new)
                w_prev = jnp.exp(b_last + m_prev - m_new)
                kw = km * w_s
                in_rows = (row_state >= j * MLSTM_DQK) & (row_state < (j + 1) * MLSTM_DQK)
                cp = jnp.where(in_rows, w_prev * cp, cp) + _dot_tn(kw.astype(BF16), vhb)
                in_lanes = (lane_n >= j * MLSTM_DQK) & (lane_n < (j + 1) * MLSTM_DQK)
                n_p = jnp.where(in_lanes, w_prev * n_p, n_p) + jnp.sum(kw, axis=0, keepdims=True)
                m_s[h] = jnp.broadcast_to(m_new, (1, LANES))
                ms = jnp.mean(hh * hh, axis=-1, keepdims=True)
                hsl = slice(h * MLSTM_DV, (h + 1) * MLSTM_DV)
                h_s[rows, hsl] = hh * lax.rsqrt(ms + HEAD_NORM_EPS) * nw_all[:, hsl]
            c_s[p] = cp
            n_s[p] = n_p
        return carry

    lax.fori_loop(0, nc, chunk_body, 0)

    y = _dot((h_s[...] * sig_o).astype(BF16), wout_ref[...])
    o_ref[...] = _layernorm(ALPHA * x + y, g_ref[...], b_ref[...])


def _mlstm_mixer(x, w_main, w_gate, conv_w, gate_b, norm_w, w_out, g, b, tb):
    seq = x.shape[0]
    wide = pltpu.VMEM((tb, D_MODEL), F32)
    return pl.pallas_call(
        functools.partial(_mlstm_kernel, tb=tb),
        out_shape=jax.ShapeDtypeStruct((seq, D_MODEL), F32),
        grid=(seq // tb,),
        in_specs=[_row_spec(tb, D_MODEL), _const_spec(w_main.shape), _const_spec(w_gate.shape),
                  _const_spec(conv_w.shape), _const_spec((1, LANES)), _const_spec((1, D_MODEL)),
                  _const_spec(w_out.shape), _const_spec((1, D_MODEL)), _const_spec((1, D_MODEL)),
                  _const_spec((tb, tb))],
        out_specs=_row_spec(tb, D_MODEL),
        scratch_shapes=[pltpu.VMEM((tb + SUB, 2 * MLSTM_QK_W), F32),
                        pltpu.VMEM((tb, MLSTM_QK_W), F32), pltpu.VMEM((tb, MLSTM_QK_W), F32),
                        wide, pltpu.VMEM((tb, LANES), F32), wide,
                        pltpu.VMEM((MLSTM_HEADS // 2, 2 * MLSTM_DQK, MLSTM_DV), F32),
                        pltpu.VMEM((MLSTM_HEADS // 2, 1, LANES), F32),
                        pltpu.VMEM((MLSTM_HEADS, 1, LANES), F32)],
        compiler_params=_params(),
        name="mlstm_mixer",
    )(x, w_main, w_gate, conv_w, gate_b, norm_w, w_out, g, b, _chunk_tri(tb))


S5_LANE_GROUPS = LANES // S5_GROUP_CH
S5_BLOCKS = D_MODEL // LANES
S5_BLOCK_STATE = S5_LANE_GROUPS * S5_STATE


def _s5_kernel(x_ref, win_ref, bre_ref, bim_ref, cre_ref, cim_ref, are_ref, aim_ref, d_ref, wout_ref,
               g_ref, b_ref, o_ref, hr_s, hi_s, y_s, *, tb):
    @pl.when(pl.program_id(0) == 0)
    def _():
        hr_s[...] = jnp.zeros_like(hr_s)
        hi_s[...] = jnp.zeros_like(hi_s)

    x = x_ref[...]
    u = _dot(x.astype(BF16), win_ref[...])
    ub = u.astype(BF16)
    row = lax.broadcasted_iota(jnp.int32, (tb, S5_BLOCK_STATE), 0)

    def shift_rows(z, k):
        if k % SUB == 0:
            return jnp.concatenate([jnp.zeros((k, z.shape[1]), F32), z[:tb - k]], axis=0)
        return jnp.where(row >= k, pltpu.roll(z, k, 0), 0.0)

    for j in range(S5_BLOCKS):
        uj = ub[:, j * LANES:(j + 1) * LANES]
        st = slice(j * S5_BLOCK_STATE, (j + 1) * S5_BLOCK_STATE)
        ar = are_ref[:, st]
        ai = aim_ref[:, st]
        hpr = hr_s[:, st]
        hpi = hi_s[:, st]
        sr = _dot(uj, bre_ref[j])
        si = _dot(uj, bim_ref[j])
        sr = sr + jnp.where(row == 0, ar * hpr - ai * hpi, 0.0)
        si = si + jnp.where(row == 0, ar * hpi + ai * hpr, 0.0)
        pr, pi = ar, ai
        k = 1
        while k < tb:
            zr = shift_rows(sr, k)
            zi = shift_rows(si, k)
            sr, si = sr + (pr * zr - pi * zi), si + (pr * zi + pi * zr)
            pr, pi = pr * pr - pi * pi, 2.0 * pr * pi
            k *= 2
        hr_s[:, st] = sr[tb - 1:tb, :]
        hi_s[:, st] = si[tb - 1:tb, :]
        y_s[:, j * LANES:(j + 1) * LANES] = _dot(sr.astype(BF16), cre_ref[j]) - _dot(si.astype(BF16), cim_ref[j])

    y = jax.nn.gelu(y_s[...] + d_ref[...] * u)
    z = _dot(y.astype(BF16), wout_ref[...])
    mix = z[:, :D_MODEL] * _sigmoid(z[:, D_MODEL:])
    o_ref[...] = _layernorm(ALPHA * x + mix, g_ref[...], b_ref[...])


def _s5_block_diag(t):
    a, b = t.shape[1], t.shape[2]
    t = t.reshape(S5_BLOCKS, S5_LANE_GROUPS, a, b)
    eye = jnp.eye(S5_LANE_GROUPS, dtype=t.dtype)
    out = jnp.einsum("jgab,gh->jgahb", t, eye)
    return out.reshape(S5_BLOCKS, S5_LANE_GROUPS * a, S5_LANE_GROUPS * b)


def _s5_mixer(x, w_in, a_re, a_im, log_dt, b_re, b_im, c_re, c_im, d_skip, w_out, g, b, tb):
    seq = x.shape[0]
    dt = jnp.exp(log_dt)[:, None]
    mag = jnp.exp(a_re * dt)
    abar_re = mag * jnp.cos(a_im * dt)
    abar_im = mag * jnp.sin(a_im * dt)
    nr = abar_re - 1.0
    ni = abar_im
    den = a_re * a_re + a_im * a_im
    coef_re = (nr * a_re + ni * a_im) / den
    coef_im = (ni * a_re - nr * a_im) / den
    bbar_re = coef_re[..., None] * b_re - coef_im[..., None] * b_im
    bbar_im = coef_re[..., None] * b_im + coef_im[..., None] * b_re
    bre = _s5_block_diag(jnp.swapaxes(bbar_re, 1, 2)).astype(BF16)
    bim = _s5_block_diag(jnp.swapaxes(bbar_im, 1, 2)).astype(BF16)
    cre = _s5_block_diag(jnp.swapaxes(c_re, 1, 2)).astype(BF16)
    cim = _s5_block_diag(jnp.swapaxes(c_im, 1, 2)).astype(BF16)
    n_state = S5_GROUPS * S5_STATE
    return pl.pallas_call(
        functools.partial(_s5_kernel, tb=tb),
        out_shape=jax.ShapeDtypeStruct((seq, D_MODEL), F32),
        grid=(seq // tb,),
        in_specs=[_row_spec(tb, D_MODEL), _const_spec(w_in.shape), _const_spec(bre.shape), _const_spec(bim.shape),
                  _const_spec(cre.shape), _const_spec(cim.shape), _const_spec((1, n_state)),
                  _const_spec((1, n_state)), _const_spec((1, D_MODEL)), _const_spec(w_out.shape),
                  _const_spec((1, D_MODEL)), _const_spec((1, D_MODEL))],
        out_specs=_row_spec(tb, D_MODEL),
        scratch_shapes=[pltpu.VMEM((1, n_state), F32), pltpu.VMEM((1, n_state), F32),
                        pltpu.VMEM((tb, D_MODEL), F32)],
        compiler_params=_params(),
        name="s5_mixer",
    )(x, w_in, bre, bim, cre, cim, abar_re.reshape(1, n_state), abar_im.reshape(1, n_state),
      d_skip.reshape(1, D_MODEL), w_out, g, b)


def _hgrn_lower_bounds(lb_logits):
    p = jax.nn.softmax(lb_logits.astype(F32), axis=0)
    c = jnp.cumsum(p, axis=0)
    return c - c[0:1]


def kernel(x, hgrn_w_in, hgrn_norm_w, hgrn_w_out, hgrn_lb_logits, mlstm_w_in, mlstm_conv_w, mlstm_gate_b, mlstm_norm_w, mlstm_w_out, s5_w_in, s5_a_re, s5_a_im, s5_log_dt, s5_b_re, s5_b_im, s5_c_re, s5_c_im, s5_d, s5_w_out, ffn_w_gate_up, ffn_w_down, ln_g, ln_b):
    bsz, seq, _ = x.shape
    tb = math.gcd(seq, ROW_BLOCK)
    assert tb % CHUNK == 0, "sequence length must be a multiple of the recurrence chunk"
    lb_all = _hgrn_lower_bounds(hgrn_lb_logits)
    row = lambda t: t.reshape(1, -1)
    outs = []
    for bi in range(bsz):
        h = x[bi]
        for i in range(DEPTH):
            kind = i % N_MIXERS
            j = i // N_MIXERS
            g0, b0 = row(ln_g[i, 0]), row(ln_b[i, 0])
            if kind == 0:
                h = _hgrn_mixer(h, hgrn_w_in[j].astype(BF16), row(hgrn_norm_w[j]), hgrn_w_out[j].astype(BF16),
                                row(lb_all[i]), g0, b0, tb)
            elif kind == 1:
                w_in = mlstm_w_in[j]
                n_main = 2 * MLSTM_QK_W + 2 * D_MODEL
                n_gate = 2 * MLSTM_HEADS
                w_gate = jnp.pad(w_in[:, n_main:], ((0, 0), (0, LANES - n_gate))).astype(BF16)
                gate_b = jnp.pad(mlstm_gate_b[j], (0, LANES - n_gate)).reshape(1, LANES)
                h = _mlstm_mixer(h, w_in[:, :n_main].astype(BF16), w_gate, mlstm_conv_w[j], gate_b,
                                 row(mlstm_norm_w[j]), mlstm_w_out[j].astype(BF16), g0, b0, tb)
            else:
                h = _s5_mixer(h, s5_w_in[j].astype(BF16), s5_a_re[j], s5_a_im[j], s5_log_dt[j], s5_b_re[j],
                              s5_b_im[j], s5_c_re[j], s5_c_im[j], s5_d[j], s5_w_out[j].astype(BF16), g0, b0, tb)
            h = _ffn(h, ffn_w_gate_up[i].astype(BF16), ffn_w_down[i].astype(BF16),
                     row(ln_g[i, 1]), row(ln_b[i, 1]), tb)
        outs.append(h)
    return jnp.stack(outs, axis=0)
```

```python
import functools
import math

import jax
import jax.numpy as jnp
from jax import lax
from jax.experimental import pallas as pl
from jax.experimental.pallas import tpu as pltpu

F32 = jnp.float32
BF16 = jnp.bfloat16

D_MODEL = 1024
DEPTH = 4
N_MIXERS = 3
ALPHA = (2.0 * DEPTH) ** 0.25
LN_EPS = 1e-5
HEAD_NORM_EPS = 1e-6

HGRN_HEADS = 8
HGRN_DK = 128
HGRN_DV = 128

MLSTM_HEADS = 8
MLSTM_DV = 128
MLSTM_DQK = 64
MLSTM_CONV = 4
MLSTM_QK_W = MLSTM_HEADS * MLSTM_DQK

S5_GROUP_CH = 16
S5_GROUPS = D_MODEL // S5_GROUP_CH
S5_STATE = 64

FFN_HIDDEN = -(-8 * D_MODEL // (3 * 256)) * 256

HGRN_HEAD_GROUP = 8
CHUNK = 64
SUB = 8
LANES = 128
ROW_BLOCK = 256
VMEM_LIMIT_BYTES = 56 * 1024 * 1024

NEG_INF = float("-inf")


def _dot(a, b):
    return jnp.dot(a, b, preferred_element_type=F32)


def _dot_nt(a, b):
    return lax.dot_general(a, b, (((1,), (1,)), ((), ())), preferred_element_type=F32)


def _dot_tn(a, b):
    return lax.dot_general(a, b, (((0,), (0,)), ((), ())), preferred_element_type=F32)


def _layernorm(y, g, b):
    mu = jnp.mean(y, axis=-1, keepdims=True)
    yc = y - mu
    var = jnp.mean(yc * yc, axis=-1, keepdims=True)
    return yc * lax.rsqrt(var + LN_EPS) * g + b


def _log1p_exp_neg(z):
    return jnp.log(1.0 + jnp.exp(-z))


def _log_sigmoid(z):
    return jnp.minimum(z, 0.0) - _log1p_exp_neg(jnp.abs(z))


def _sigmoid(z):
    return 1.0 / (1.0 + jnp.exp(-z))


def _chunk_cumsum(tri, z):
    hi = z.astype(BF16)
    r1 = z - hi.astype(F32)
    mid = r1.astype(BF16)
    lo = (r1 - mid.astype(F32)).astype(BF16)
    return _dot(tri, hi) + _dot(tri, mid) + _dot(tri, lo)


def _const_spec(shape):
    nd = len(shape)
    return pl.BlockSpec(shape, lambda i: (0,) * nd, pipeline_mode=pl.Buffered(1))


def _row_spec(tb, width):
    return pl.BlockSpec((tb, width), lambda i: (i, 0))


def _pipeline_specs(tb, width, nb):
    front = pl.BlockSpec((tb, width), lambda i: (jnp.minimum(i, nb - 1), 0))
    back = pl.BlockSpec((tb, width), lambda i: (jnp.maximum(i - 1, 0), 0))
    return front, back


def _params():
    return pltpu.CompilerParams(dimension_semantics=("arbitrary",), vmem_limit_bytes=VMEM_LIMIT_BYTES)


def _ffn_kernel(x_ref, wgu_ref, wd_ref, g_ref, b_ref, o_ref):
    x = x_ref[...]
    gu = _dot(x.astype(BF16), wgu_ref[...])
    gate = gu[:, :FFN_HIDDEN]
    up = gu[:, FFN_HIDDEN:]
    act = (gate * _sigmoid(gate) * up).astype(BF16)
    y = _dot(act, wd_ref[...])
    o_ref[...] = _layernorm(ALPHA * x + y, g_ref[...], b_ref[...])


def _ffn(x, w_gate_up, w_down, g, b, tb):
    seq = x.shape[0]
    return pl.pallas_call(
        _ffn_kernel,
        out_shape=jax.ShapeDtypeStruct((seq, D_MODEL), F32),
        grid=(seq // tb,),
        in_specs=[_row_spec(tb, D_MODEL), _const_spec(w_gate_up.shape), _const_spec(w_down.shape),
                  _const_spec((1, D_MODEL)), _const_spec((1, D_MODEL))],
        out_specs=_row_spec(tb, D_MODEL),
        compiler_params=_params(),
        name="ffn",
    )(x, w_gate_up, w_down, g, b)


def _hgrn_kernel(xf_ref, xb_ref, win_ref, lb_ref, nw_ref, wout_ref, g_ref, b_ref, tri_ref, o_ref,
                 q_s, k_s, v_s, b_s, gate_s, o_s, state_s, *, tb):
    nc = tb // CHUNK

    @pl.when(pl.program_id(0) == 0)
    def _():
        state_s[...] = jnp.zeros_like(state_s)
        for ref in (q_s, k_s, v_s, b_s, gate_s):
            ref[...] = jnp.zeros_like(ref)

    proj = _dot(xf_ref[...].astype(BF16), win_ref[...])
    q = proj[:, 0:D_MODEL]
    f = proj[:, D_MODEL:2 * D_MODEL]
    v = proj[:, 2 * D_MODEL:3 * D_MODEL]
    gt = proj[:, 3 * D_MODEL:4 * D_MODEL]
    lb = lb_ref[...]
    a1 = jnp.log(lb)
    a2 = jnp.log1p(-lb) + _log_sigmoid(f)
    log_f = jnp.maximum(a1, a2) + _log1p_exp_neg(jnp.abs(a1 - a2))
    k = (1.0 - lb) * _sigmoid(-f)
    q = q * _sigmoid(q)
    gate = gt * _sigmoid(gt)
    bcum = _chunk_cumsum(tri_ref[...], log_f)

    row = lax.broadcasted_iota(jnp.int32, (CHUNK, LANES), 0)
    rowi = lax.broadcasted_iota(jnp.int32, (CHUNK, CHUNK), 0)
    coli = lax.broadcasted_iota(jnp.int32, (CHUNK, CHUNK), 1)
    row8 = lax.broadcasted_iota(jnp.int32, (SUB, LANES), 0)
    lane8 = lax.broadcasted_iota(jnp.int32, (SUB, CHUNK), 1)
    same_block = {}
    m = SUB
    while m < CHUNK:
        shift = int(math.log2(2 * m))
        same_block[m] = ((rowi >> shift) == (coli >> shift)).astype(F32)
        m *= 2
    nw = nw_ref[...]

    def head_chunk(h, rows):
        st = state_s[h]
        qc = q_s[h, rows, :]
        kc = k_s[h, rows, :]
        vc = v_s[h, rows, :]
        bc = b_s[h, rows, :]
        b_last = bc[CHUNK - 1:CHUNK, :]
        q_dec = (qc * jnp.exp(bc)).astype(BF16)
        att = jnp.zeros((CHUNK, CHUNK), F32)
        m = SUB
        while m < CHUNK:
            ref = jnp.concatenate(
                [jnp.broadcast_to(bc[blk * 2 * m + m - 1:blk * 2 * m + m, :], (2 * m, LANES))
                 for blk in range(CHUNK // (2 * m))], axis=0)
            upper = (row & m) != 0
            qh = qc * jnp.exp(jnp.where(upper, bc - ref, NEG_INF))
            kh = kc * jnp.exp(jnp.where(upper, NEG_INF, ref - bc))
            pair = _dot_nt(qh.astype(BF16), kh.astype(BF16))
            att = att + pair * same_block[m]
            m *= 2
        tiles = []
        for blk in range(CHUNK // SUB):
            sl = slice(blk * SUB, (blk + 1) * SUB)
            qi, bi = qc[sl], bc[sl]
            tile = jnp.zeros((SUB, CHUNK), F32)
            for s in range(SUB):
                r = rows.start + blk * SUB + s
                ks = jnp.broadcast_to(k_s[h, r:r + 1, :], (SUB, LANES))
                bs = jnp.broadcast_to(b_s[h, r:r + 1, :], (SUB, LANES))
                decay = jnp.exp(jnp.where(row8 >= s, bi - bs, NEG_INF))
                col = jnp.sum(qi * (ks * decay), axis=1, keepdims=True)
                tile = jnp.where(lane8 == blk * SUB + s, col, tile)
            tiles.append(tile)
        att = att + jnp.concatenate(tiles, axis=0)
        o = _dot(att.astype(BF16), vc.astype(BF16)) + _dot_nt(q_dec, st.astype(BF16))
        k_dec = kc * jnp.exp(b_last - bc)
        state_s[h] = st * jnp.exp(b_last) + _dot_tn(vc.astype(BF16), k_dec.astype(BF16))
        o_s[h, rows, :] = o

    for c in range(nc):
        for h in range(HGRN_HEADS):
            head_chunk(h, slice(c * CHUNK, (c + 1) * CHUNK))

    normed = []
    for h in range(HGRN_HEADS):
        o = o_s[h]
        ms = jnp.mean(o * o, axis=-1, keepdims=True)
        normed.append((o * lax.rsqrt(ms + HEAD_NORM_EPS) * nw * gate_s[h]).astype(BF16))
    y = _dot(jnp.concatenate(normed, axis=1), wout_ref[...])
    o_ref[...] = _layernorm(ALPHA * xb_ref[...] + y, g_ref[...], b_ref[...])

    for h in range(HGRN_HEADS):
        sl = slice(h * HGRN_DK, (h + 1) * HGRN_DK)
        q_s[h] = q[:, sl]
        k_s[h] = k[:, sl]
        v_s[h] = v[:, sl]
        b_s[h] = bcum[:, sl]
        gate_s[h] = gate[:, sl]


def _chunk_tri(tb):
    i = jnp.arange(tb)
    same = (i[:, None] // CHUNK) == (i[None, :] // CHUNK)
    return (same & (i[None, :] <= i[:, None])).astype(BF16)


def _hgrn_mixer(x, w_in, norm_w, w_out, lb, g, b, tb):
    seq = x.shape[0]
    nb = seq // tb
    head_scratch = pltpu.VMEM((HGRN_HEADS, tb, HGRN_DK), F32)
    front_spec, back_spec = _pipeline_specs(tb, D_MODEL, nb)
    return pl.pallas_call(
        functools.partial(_hgrn_kernel, tb=tb),
        out_shape=jax.ShapeDtypeStruct((seq, D_MODEL), F32),
        grid=(nb + 1,),
        in_specs=[front_spec, back_spec, _const_spec(w_in.shape), _const_spec((1, D_MODEL)),
                  _const_spec((1, HGRN_DV)), _const_spec(w_out.shape), _const_spec((1, D_MODEL)),
                  _const_spec((1, D_MODEL)), _const_spec((tb, tb))],
        out_specs=back_spec,
        scratch_shapes=[head_scratch] * 6 + [pltpu.VMEM((HGRN_HEADS, HGRN_DV, HGRN_DK), F32)],
        compiler_params=_params(),
        name="hgrn_mixer",
    )(x, x, w_in, lb, norm_w, w_out, g, b, _chunk_tri(tb))


def _mlstm_kernel(x_ref, win_ref, wg_ref, conv_ref, gb_ref, nw_ref, wout_ref, g_ref, b_ref, tri_ref, sel_ref, o_ref,
                  pad_s, q_s, k_s, v_s, crow_s, gcol_s, wi_s, em_s, ws_s, wp_s, h_s, c_s, m_s, *, tb):
    nc = tb // CHUNK
    halo = SUB

    @pl.when(pl.program_id(0) == 0)
    def _():
        pad_s[0:halo, :] = jnp.zeros((halo, 2 * MLSTM_QK_W), F32)
        c_s[...] = jnp.zeros_like(c_s)
        m_s[...] = jnp.zeros_like(m_s)

    x = x_ref[...]
    xb = x.astype(BF16)
    proj = _dot(xb, win_ref[...])
    v_s[...] = proj[:, 2 * MLSTM_QK_W:2 * MLSTM_QK_W + D_MODEL]
    o_pre = proj[:, 2 * MLSTM_QK_W + D_MODEL:]
    pad_s[halo:halo + tb, :] = proj[:, :2 * MLSTM_QK_W]
    conv_w = conv_ref[...]
    qk = pad_s[pl.ds(halo - MLSTM_CONV + 1, tb), :] * conv_w[0:1, :]
    for j in range(1, MLSTM_CONV):
        qk = qk + pad_s[pl.ds(halo - MLSTM_CONV + 1 + j, tb), :] * conv_w[j:j + 1, :]
    pad_s[0:halo, :] = pad_s[tb:tb + halo, :]
    qk = qk * _sigmoid(qk)
    q_s[...] = qk[:, :MLSTM_QK_W]
    k_s[...] = qk[:, MLSTM_QK_W:] * (MLSTM_DQK ** -0.5)
    gates = _dot(xb, wg_ref[...]) + gb_ref[...]
    li = gates[:, :LANES]
    bcum = _chunk_cumsum(tri_ref[...], _log_sigmoid(gates[:, LANES:]))
    cq = li - bcum
    rin = lax.broadcasted_iota(jnp.int32, (tb, LANES), 0) & (CHUNK - 1)
    cm = cq
    k = 1
    while k < CHUNK:
        shifted = pltpu.roll(cm, k, 0)
        cm = jnp.maximum(cm, jnp.where(rin >= k, shifted, NEG_INF))
        k *= 2
    m_prev = m_s[...]
    g_parts, wi_parts, em_parts, ws_parts, wp_parts = [], [], [], [], []
    for c in range(nc):
        sl = slice(c * CHUNK, (c + 1) * CHUNK)
        mx = jnp.maximum(m_prev, cm[(c + 1) * CHUNK - 1:(c + 1) * CHUNK])
        g_c = jnp.maximum(cm[sl], m_prev)
        g_parts.append(g_c)
        wi_parts.append(jnp.exp(m_prev - g_c))
        em_parts.append(jnp.exp(-(bcum[sl] + g_c)))
        ws_parts.append(jnp.exp(cq[sl] - mx))
        wp_parts.append(jnp.exp(m_prev - mx))
        m_prev = bcum[(c + 1) * CHUNK - 1:(c + 1) * CHUNK] + mx
        crow_s[c] = cq[sl].T[0:SUB, :]
    m_s[...] = m_prev
    wp_parts.append(jnp.zeros((SUB - nc, LANES), F32))
    stack = jnp.concatenate(g_parts + wi_parts + em_parts + ws_parts + wp_parts, axis=0)
    hi = stack.astype(BF16)
    r1 = stack - hi.astype(F32)
    mid = r1.astype(BF16)
    lo = (r1 - mid.astype(F32)).astype(BF16)
    spread = _dot(jnp.concatenate([hi, mid, lo], axis=1), sel_ref[...])
    gcol_s[...] = spread[0:tb]
    wi_s[...] = spread[tb:2 * tb]
    em_s[...] = spread[2 * tb:3 * tb]
    ws_s[...] = spread[3 * tb:4 * tb]
    wp_s[...] = spread[4 * tb:4 * tb + SUB]

    lane_c = lax.broadcasted_iota(jnp.int32, (CHUNK, LANES), 1)
    rowi = lax.broadcasted_iota(jnp.int32, (CHUNK, CHUNK), 0)
    coli = lax.broadcasted_iota(jnp.int32, (CHUNK, CHUNK), 1)
    causal = rowi >= coli
    first_head = lane_c < MLSTM_DQK
    first_rows = lax.broadcasted_iota(jnp.int32, (2 * MLSTM_DQK, 2 * MLSTM_DV), 0) < MLSTM_DQK
    ones_blk = jnp.ones((CHUNK, LANES), BF16)
    nw_all = nw_ref[...]
    sig_o = _sigmoid(o_pre)

    def chunk_body(c, carry):
        rows = pl.ds(pl.multiple_of(c * CHUNK, CHUNK), CHUNK)
        crow = crow_s[c]
        wp_row = wp_s[pl.ds(c, 1), :]
        for p in range(MLSTM_HEADS // 2):
            lanes_p = slice(p * LANES, (p + 1) * LANES)
            cext = c_s[p]
            q_pair = q_s[rows, lanes_p]
            k_pair = k_s[rows, lanes_p]
            hsl = [slice((2 * p + j) * MLSTM_DV, (2 * p + j + 1) * MLSTM_DV) for j in range(2)]
            qm = jnp.concatenate([jnp.where(first_head, q_pair, 0.0), jnp.where(first_head, 0.0, q_pair)],
                                 axis=0).astype(BF16)
            v_ext = [jnp.concatenate([v_s[rows, hsl[j]].astype(BF16), ones_blk], axis=1) for j in range(2)]
            scores = _dot_nt(qm, k_pair.astype(BF16))
            inter = _dot(qm, cext.astype(BF16))
            kw = jnp.concatenate([jnp.where(first_head, k_pair * ws_s[rows, hsl[0]], 0.0),
                                  jnp.where(first_head, 0.0, k_pair * ws_s[rows, hsl[1]])], axis=0).astype(BF16)
            update = _dot_tn(kw, jnp.concatenate(v_ext, axis=0))
            w_prev = jnp.where(first_rows, jnp.tile(wp_row[:, hsl[0]], (1, 2)), jnp.tile(wp_row[:, hsl[1]], (1, 2)))
            c_s[p] = w_prev * cext + update
            for j in range(2):
                h = 2 * p + j
                hr = slice(j * CHUNK, (j + 1) * CHUNK)
                d_mat = jnp.exp(jnp.where(causal, crow[h:h + 1, :] - gcol_s[rows, h * MLSTM_DV:h * MLSTM_DV + CHUNK],
                                          NEG_INF))
                s = scores[hr] * d_mat
                both = _dot(s.astype(BF16), v_ext[j]) + jnp.tile(wi_s[rows, hsl[j]], (1, 2)) * inter[hr]
                h_s[rows, hsl[j]] = both[:, :MLSTM_DV] / jnp.maximum(jnp.abs(both[:, MLSTM_DV:]), em_s[rows, hsl[j]])
        return carry

    lax.fori_loop(0, nc, chunk_body, 0)

    normed = []
    for h in range(MLSTM_HEADS):
        hsl = slice(h * MLSTM_DV, (h + 1) * MLSTM_DV)
        hh = h_s[:, hsl]
        ms = jnp.mean(hh * hh, axis=-1, keepdims=True)
        normed.append((hh * lax.rsqrt(ms + HEAD_NORM_EPS) * nw_all[:, hsl] * sig_o[:, hsl]).astype(BF16))
    y = _dot(jnp.concatenate(normed, axis=1), wout_ref[...])
    o_ref[...] = _layernorm(ALPHA * x + y, g_ref[...], b_ref[...])


def _mlstm_split_weights(w_in, gate_b):
    n_main = 2 * MLSTM_QK_W + 2 * D_MODEL
    pad = LANES - MLSTM_HEADS
    w_i = jnp.pad(w_in[:, n_main:n_main + MLSTM_HEADS], ((0, 0), (0, pad)))
    w_f = jnp.pad(w_in[:, n_main + MLSTM_HEADS:], ((0, 0), (0, pad)))
    b_i = jnp.pad(gate_b[:MLSTM_HEADS], (0, pad))
    b_f = jnp.pad(gate_b[MLSTM_HEADS:], (0, pad))
    return (w_in[:, :n_main].astype(BF16), jnp.concatenate([w_i, w_f], axis=1).astype(BF16),
            jnp.concatenate([b_i, b_f]).reshape(1, 2 * LANES))


def _lane_spread_matrix():
    src = jnp.arange(LANES)[:, None]
    dst = jnp.arange(D_MODEL)[None, :] // LANES
    return jnp.tile((src == dst).astype(BF16), (3, 1))


def _mlstm_mixer(x, w_main, w_gate, conv_w, gate_b, norm_w, w_out, g, b, tb):
    seq = x.shape[0]
    assert tb // CHUNK <= SUB
    wide = pltpu.VMEM((tb, D_MODEL), F32)
    sel = _lane_spread_matrix()
    return pl.pallas_call(
        functools.partial(_mlstm_kernel, tb=tb),
        out_shape=jax.ShapeDtypeStruct((seq, D_MODEL), F32),
        grid=(seq // tb,),
        in_specs=[_row_spec(tb, D_MODEL), _const_spec(w_main.shape), _const_spec(w_gate.shape),
                  _const_spec(conv_w.shape), _const_spec((1, 2 * LANES)), _const_spec((1, D_MODEL)),
                  _const_spec(w_out.shape), _const_spec((1, D_MODEL)), _const_spec((1, D_MODEL)),
                  _const_spec((tb, tb)), _const_spec(sel.shape)],
        out_specs=_row_spec(tb, D_MODEL),
        scratch_shapes=[pltpu.VMEM((tb + SUB, 2 * MLSTM_QK_W), F32),
                        pltpu.VMEM((tb, MLSTM_QK_W), F32), pltpu.VMEM((tb, MLSTM_QK_W), F32),
                        wide, pltpu.VMEM((tb // CHUNK, SUB, CHUNK), F32), wide, wide, wide, wide,
                        pltpu.VMEM((SUB, D_MODEL), F32), wide,
                        pltpu.VMEM((MLSTM_HEADS // 2, 2 * MLSTM_DQK, 2 * MLSTM_DV), F32),
                        pltpu.VMEM((1, LANES), F32)],
        compiler_params=_params(),
        name="mlstm_mixer",
    )(x, w_main, w_gate, conv_w, gate_b, norm_w, w_out, g, b, _chunk_tri(tb), sel)


S5_LANE_GROUPS = LANES // S5_GROUP_CH
S5_BLOCKS = D_MODEL // LANES
S5_BLOCK_STATE = S5_LANE_GROUPS * S5_STATE


def _s5_kernel(x_ref, win_ref, bre_ref, bim_ref, cre_ref, cim_ref, are_ref, aim_ref, d_ref, wout_ref,
               g_ref, b_ref, o_ref, hr_s, hi_s, y_s, *, tb):
    @pl.when(pl.program_id(0) == 0)
    def _():
        hr_s[...] = jnp.zeros_like(hr_s)
        hi_s[...] = jnp.zeros_like(hi_s)

    x = x_ref[...]
    u = _dot(x.astype(BF16), win_ref[...])
    ub = u.astype(BF16)
    nt = tb // SUB
    row8 = lax.broadcasted_iota(jnp.int32, (SUB, S5_BLOCK_STATE), 0)

    def tile_scan(zr, zi, mults):
        for i, (mr, mi) in enumerate(mults):
            rr = pltpu.roll(zr, 1 << i, 1)
            ri = pltpu.roll(zi, 1 << i, 1)
            zr, zi = zr + (mr * rr - mi * ri), zi + (mr * ri + mi * rr)
        return zr, zi

    for j in range(S5_BLOCKS):
        uj = ub[:, j * LANES:(j + 1) * LANES]
        st = slice(j * S5_BLOCK_STATE, (j + 1) * S5_BLOCK_STATE)
        ar = are_ref[:, st]
        ai = aim_ref[:, st]
        mults = []
        pr, pi = ar, ai
        for i in range(3):
            keep = row8 >= (1 << i)
            mults.append((jnp.where(keep, pr, 0.0), jnp.where(keep, pi, 0.0)))
            pr, pi = pr * pr - pi * pi, 2.0 * pr * pi
        pwr, pwi = tile_scan(jnp.where(row8 == 0, ar, 0.0)[None], jnp.where(row8 == 0, ai, 0.0)[None], mults)
        pwr, pwi = pwr[0], pwi[0]
        sr = _dot(uj, bre_ref[j]).reshape(nt, SUB, S5_BLOCK_STATE)
        si = _dot(uj, bim_ref[j]).reshape(nt, SUB, S5_BLOCK_STATE)
        sr, si = tile_scan(sr, si, mults)
        cr = hr_s[:, st]
        ci = hi_s[:, st]
        tiles_r, tiles_i = [], []
        for n in range(nt):
            tr = sr[n] + (pwr * cr - pwi * ci)
            ti = si[n] + (pwr * ci + pwi * cr)
            tiles_r.append(tr)
            tiles_i.append(ti)
            cr, ci = tr[SUB - 1:SUB, :], ti[SUB - 1:SUB, :]
        hr_s[:, st] = cr
        hi_s[:, st] = ci
        hr = jnp.concatenate(tiles_r, axis=0)
        hi = jnp.concatenate(tiles_i, axis=0)
        y_s[:, j * LANES:(j + 1) * LANES] = _dot(hr.astype(BF16), cre_ref[j]) - _dot(hi.astype(BF16), cim_ref[j])

    y = jax.nn.gelu(y_s[...] + d_ref[...] * u)
    z = _dot(y.astype(BF16), wout_ref[...])
    mix = z[:, :D_MODEL] * _sigmoid(z[:, D_MODEL:])
    o_ref[...] = _layernorm(ALPHA * x + mix, g_ref[...], b_ref[...])


def _s5_block_diag(t):
    a, b = t.shape[1], t.shape[2]
    t = t.reshape(S5_BLOCKS, S5_LANE_GROUPS, a, b)
    eye = jnp.eye(S5_LANE_GROUPS, dtype=t.dtype)
    out = jnp.einsum("jgab,gh->jgahb", t, eye)
    return out.reshape(S5_BLOCKS, S5_LANE_GROUPS * a, S5_LANE_GROUPS * b)


def _s5_mixer(x, w_in, a_re, a_im, log_dt, b_re, b_im, c_re, c_im, d_skip, w_out, g, b, tb):
    seq = x.shape[0]
    dt = jnp.exp(log_dt)[:, None]
    mag = jnp.exp(a_re * dt)
    abar_re = mag * jnp.cos(a_im * dt)
    abar_im = mag * jnp.sin(a_im * dt)
    nr = abar_re - 1.0
    ni = abar_im
    den = a_re * a_re + a_im * a_im
    coef_re = (nr * a_re + ni * a_im) / den
    coef_im = (ni * a_re - nr * a_im) / den
    bbar_re = coef_re[..., None] * b_re - coef_im[..., None] * b_im
    bbar_im = coef_re[..., None] * b_im + coef_im[..., None] * b_re
    bre = _s5_block_diag(jnp.swapaxes(bbar_re, 1, 2)).astype(BF16)
    bim = _s5_block_diag(jnp.swapaxes(bbar_im, 1, 2)).astype(BF16)
    cre = _s5_block_diag(jnp.swapaxes(c_re, 1, 2)).astype(BF16)
    cim = _s5_block_diag(jnp.swapaxes(c_im, 1, 2)).astype(BF16)
    n_state = S5_GROUPS * S5_STATE
    return pl.pallas_call(
        functools.partial(_s5_kernel, tb=tb),
        out_shape=jax.ShapeDtypeStruct((seq, D_MODEL), F32),
        grid=(seq // tb,),
        in_specs=[_row_spec(tb, D_MODEL), _const_spec(w_in.shape), _const_spec(bre.shape), _const_spec(bim.shape),
                  _const_spec(cre.shape), _const_spec(cim.shape), _const_spec((1, n_state)),
                  _const_spec((1, n_state)), _const_spec((1, D_MODEL)), _const_spec(w_out.shape),
                  _const_spec((1, D_MODEL)), _const_spec((1, D_MODEL))],
        out_specs=_row_spec(tb, D_MODEL),
        scratch_shapes=[pltpu.VMEM((1, n_state), F32), pltpu.VMEM((1, n_state), F32),
                        pltpu.VMEM((tb, D_MODEL), F32)],
        compiler_params=_params(),
        name="s5_mixer",
    )(x, w_in, bre, bim, cre, cim, abar_re.reshape(1, n_state), abar_im.reshape(1, n_state),
      d_skip.reshape(1, D_MODEL), w_out, g, b)


def _hgrn_lower_bounds(lb_logits):
    p = jax.nn.softmax(lb_logits.astype(F32), axis=0)
    c = jnp.cumsum(p, axis=0)
    return c - c[0:1]


def kernel(x, hgrn_w_in, hgrn_norm_w, hgrn_w_out, hgrn_lb_logits, mlstm_w_in, mlstm_conv_w, mlstm_gate_b, mlstm_norm_w, mlstm_w_out, s5_w_in, s5_a_re, s5_a_im, s5_log_dt, s5_b_re, s5_b_im, s5_c_re, s5_c_im, s5_d, s5_w_out, ffn_w_gate_up, ffn_w_down, ln_g, ln_b):
    bsz, seq, _ = x.shape
    tb = math.gcd(seq, ROW_BLOCK)
    assert tb % CHUNK == 0, "sequence length must be a multiple of the recurrence chunk"
    lb_all = _hgrn_lower_bounds(hgrn_lb_logits)
    row = lambda t: t.reshape(1, -1)
    outs = []
    for bi in range(bsz):
        h = x[bi]
        for i in range(DEPTH):
            kind = i % N_MIXERS
            j = i // N_MIXERS
            g0, b0 = row(ln_g[i, 0]), row(ln_b[i, 0])
            if kind == 0:
                h = _hgrn_mixer(h, hgrn_w_in[j].astype(BF16), row(hgrn_norm_w[j]), hgrn_w_out[j].astype(BF16),
                                row(lb_all[i]), g0, b0, tb)
            elif kind == 1:
                w_main, w_gate, gate_b = _mlstm_split_weights(mlstm_w_in[j], mlstm_gate_b[j])
                h = _mlstm_mixer(h, w_main, w_gate, mlstm_conv_w[j], gate_b,
                                 row(mlstm_norm_w[j]), mlstm_w_out[j].astype(BF16), g0, b0, tb)
            else:
                h = _s5_mixer(h, s5_w_in[j].astype(BF16), s5_a_re[j], s5_a_im[j], s5_log_dt[j], s5_b_re[j],
                              s5_b_im[j], s5_c_re[j], s5_c_im[j], s5_d[j], s5_w_out[j].astype(BF16), g0, b0, tb)
            h = _ffn(h, ffn_w_gate_up[i].astype(BF16), ffn_w_down[i].astype(BF16),
                     row(ln_g[i, 1]), row(ln_b[i, 1]), tb)
        outs.append(h)
    return jnp.stack(outs, axis=0)
```

```python
import functools
import math

import jax
import jax.numpy as jnp
from jax import lax
from jax.experimental import pallas as pl
from jax.experimental.pallas import tpu as pltpu

F32 = jnp.float32
BF16 = jnp.bfloat16

D_MODEL = 1024
DEPTH = 4
N_MIXERS = 3
ALPHA = (2.0 * DEPTH) ** 0.25
LN_EPS = 1e-5
HEAD_NORM_EPS = 1e-6

HGRN_HEADS = 8
HGRN_DK = 128
HGRN_DV = 128

MLSTM_HEADS = 8
MLSTM_DV = 128
MLSTM_DQK = 64
MLSTM_CONV = 4
MLSTM_QK_W = MLSTM_HEADS * MLSTM_DQK

S5_GROUP_CH = 16
S5_GROUPS = D_MODEL // S5_GROUP_CH
S5_STATE = 64

FFN_HIDDEN = -(-8 * D_MODEL // (3 * 256)) * 256

HGRN_HEAD_GROUP = 8
CHUNK = 64
SUB = 8
LANES = 128
ROW_BLOCK = 256
VMEM_LIMIT_BYTES = 60 * 1024 * 1024

NEG_INF = float("-inf")


def _dot(a, b):
    return jnp.dot(a, b, preferred_element_type=F32)


def _dot_nt(a, b):
    return lax.dot_general(a, b, (((1,), (1,)), ((), ())), preferred_element_type=F32)


def _dot_tn(a, b):
    return lax.dot_general(a, b, (((0,), (0,)), ((), ())), preferred_element_type=F32)


def _layernorm(y, g, b):
    mu = jnp.mean(y, axis=-1, keepdims=True)
    yc = y - mu
    var = jnp.mean(yc * yc, axis=-1, keepdims=True)
    return yc * lax.rsqrt(var + LN_EPS) * g + b


def _log1p_exp_neg(z):
    return jnp.log(1.0 + jnp.exp(-z))


def _log_sigmoid(z):
    return jnp.minimum(z, 0.0) - _log1p_exp_neg(jnp.abs(z))


def _sigmoid(z):
    return 1.0 / (1.0 + jnp.exp(-z))


def _chunk_cumsum(tri, z):
    hi = z.astype(BF16)
    r1 = z - hi.astype(F32)
    mid = r1.astype(BF16)
    lo = (r1 - mid.astype(F32)).astype(BF16)
    return _dot(tri, hi) + _dot(tri, mid) + _dot(tri, lo)


def _const_spec(shape):
    nd = len(shape)
    return pl.BlockSpec(shape, lambda i: (0,) * nd, pipeline_mode=pl.Buffered(1))


def _layer_spec(stacked_shape, layer, cols=None):
    _, rows, width = stacked_shape
    return pl.BlockSpec((None, rows, cols or width), lambda i: (layer, 0, 0), pipeline_mode=pl.Buffered(1))


def _pipeline_specs(tb, width, nb, stages):
    return [pl.BlockSpec((tb, width), functools.partial(lambda i, s: (jnp.clip(i - s, 0, nb - 1), 0), s=s))
            for s in range(stages)]


def _params():
    return pltpu.CompilerParams(dimension_semantics=("arbitrary",), vmem_limit_bytes=VMEM_LIMIT_BYTES)


FFN_PIECE = 256


class _FfnPieces:
    def __init__(self, x, wgu_ref, wd_ref):
        self.x = x
        self.xb = x.astype(BF16)
        self.wgu_ref = wgu_ref
        self.wd_ref = wd_ref
        self.done = 0
        self.acc = None

    def advance_to(self, fraction):
        target = min(FFN_HIDDEN // FFN_PIECE, int(fraction * (FFN_HIDDEN // FFN_PIECE) + 1e-9))
        while self.done < target:
            lo = self.done * FFN_PIECE
            gate = _dot(self.xb, self.wgu_ref[:, lo:lo + FFN_PIECE])
            up = _dot(self.xb, self.wgu_ref[:, FFN_HIDDEN + lo:FFN_HIDDEN + lo + FFN_PIECE])
            act = (gate * _sigmoid(gate) * up).astype(BF16)
            part = _dot(act, self.wd_ref[lo:lo + FFN_PIECE, :])
            self.acc = part if self.acc is None else self.acc + part
            self.done += 1

    def finish(self, g_ref, b_ref):
        self.advance_to(1.0)
        return _layernorm(ALPHA * self.x + self.acc, g_ref[...], b_ref[...])


def _hgrn_kernel(xf_ref, xb_ref, win_ref, lb_ref, nw_ref, wout_ref, g_ref, b_ref, tri_ref,
                 wgu_ref, wd_ref, g2_ref, b2_ref, o_ref,
                 q_s, k_s, v_s, b_s, gate_s, o_s, state_s, x1_s, *, tb):
    nc = tb // CHUNK

    @pl.when(pl.program_id(0) == 0)
    def _():
        state_s[...] = jnp.zeros_like(state_s)
        for ref in (q_s, k_s, v_s, b_s, gate_s, x1_s):
            ref[...] = jnp.zeros_like(ref)

    ffn = _FfnPieces(x1_s[...], wgu_ref, wd_ref)

    proj = _dot(xf_ref[...].astype(BF16), win_ref[...])
    q = proj[:, 0:D_MODEL]
    f = proj[:, D_MODEL:2 * D_MODEL]
    v = proj[:, 2 * D_MODEL:3 * D_MODEL]
    gt = proj[:, 3 * D_MODEL:4 * D_MODEL]
    lb = lb_ref[...]
    a1 = jnp.log(lb)
    a2 = jnp.log1p(-lb) + _log_sigmoid(f)
    log_f = jnp.maximum(a1, a2) + _log1p_exp_neg(jnp.abs(a1 - a2))
    k = (1.0 - lb) * _sigmoid(-f)
    q = q * _sigmoid(q)
    gate = gt * _sigmoid(gt)
    bcum = _chunk_cumsum(tri_ref[...], log_f)

    row = lax.broadcasted_iota(jnp.int32, (CHUNK, LANES), 0)
    rowi = lax.broadcasted_iota(jnp.int32, (CHUNK, CHUNK), 0)
    coli = lax.broadcasted_iota(jnp.int32, (CHUNK, CHUNK), 1)
    row8 = lax.broadcasted_iota(jnp.int32, (SUB, LANES), 0)
    lane8 = lax.broadcasted_iota(jnp.int32, (SUB, CHUNK), 1)
    same_block = {}
    m = SUB
    while m < CHUNK:
        shift = int(math.log2(2 * m))
        same_block[m] = ((rowi >> shift) == (coli >> shift)).astype(F32)
        m *= 2
    nw = nw_ref[...]

    def head_scores(h, rows):
        qc = q_s[h, rows, :]
        kc = k_s[h, rows, :]
        bc = b_s[h, rows, :]
        att = jnp.zeros((CHUNK, CHUNK), F32)
        m = SUB
        while m < CHUNK:
            ref = jnp.concatenate(
                [jnp.broadcast_to(bc[blk * 2 * m + m - 1:blk * 2 * m + m, :], (2 * m, LANES))
                 for blk in range(CHUNK // (2 * m))], axis=0)
            upper = (row & m) != 0
            qh = qc * jnp.exp(jnp.where(upper, bc - ref, NEG_INF))
            kh = kc * jnp.exp(jnp.where(upper, NEG_INF, ref - bc))
            pair = _dot_nt(qh.astype(BF16), kh.astype(BF16))
            att = att + pair * same_block[m]
            m *= 2
        tiles = []
        for blk in range(CHUNK // SUB):
            sl = slice(blk * SUB, (blk + 1) * SUB)
            qi, bi = qc[sl], bc[sl]
            tile = jnp.zeros((SUB, CHUNK), F32)
            for s in range(SUB):
                r = rows.start + blk * SUB + s
                ks = jnp.broadcast_to(k_s[h, r:r + 1, :], (SUB, LANES))
                bs = jnp.broadcast_to(b_s[h, r:r + 1, :], (SUB, LANES))
                decay = jnp.exp(jnp.where(row8 >= s, bi - bs, NEG_INF))
                col = jnp.sum(qi * (ks * decay), axis=1, keepdims=True)
                tile = jnp.where(lane8 == blk * SUB + s, col, tile)
            tiles.append(tile)
        return (att + jnp.concatenate(tiles, axis=0)).astype(BF16)

    def head_output(h, rows, att):
        st = state_s[h]
        qc = q_s[h, rows, :]
        kc = k_s[h, rows, :]
        vcb = v_s[h, rows, :].astype(BF16)
        bc = b_s[h, rows, :]
        b_last = bc[CHUNK - 1:CHUNK, :]
        o_s[h, rows, :] = _dot(att, vcb) + _dot_nt((qc * jnp.exp(bc)).astype(BF16), st.astype(BF16))
        k_dec = kc * jnp.exp(b_last - bc)
        state_s[h] = st * jnp.exp(b_last) + _dot_tn(vcb, k_dec.astype(BF16))

    n_units = 2 * nc * HGRN_HEADS
    unit = 0
    for c in range(nc):
        rows = slice(c * CHUNK, (c + 1) * CHUNK)
        atts = []
        for h in range(HGRN_HEADS):
            atts.append(head_scores(h, rows))
            unit += 1
            ffn.advance_to(unit / n_units)
        for h in range(HGRN_HEADS):
            head_output(h, rows, atts[h])
            unit += 1
            ffn.advance_to(unit / n_units)
    o_ref[...] = ffn.finish(g2_ref, b2_ref)

    normed = []
    for h in range(HGRN_HEADS):
        o = o_s[h]
        ms = jnp.mean(o * o, axis=-1, keepdims=True)
        normed.append((o * lax.rsqrt(ms + HEAD_NORM_EPS) * nw * gate_s[h]).astype(BF16))
    y = _dot(jnp.concatenate(normed, axis=1), wout_ref[...])
    x1_new = _layernorm(ALPHA * xb_ref[...] + y, g_ref[...], b_ref[...])

    x1_s[...] = x1_new
    for h in range(HGRN_HEADS):
        sl = slice(h * HGRN_DK, (h + 1) * HGRN_DK)
        q_s[h] = q[:, sl]
        k_s[h] = k[:, sl]
        v_s[h] = v[:, sl]
        b_s[h] = bcum[:, sl]
        gate_s[h] = gate[:, sl]


def _chunk_tri(tb):
    i = jnp.arange(tb)
    same = (i[:, None] // CHUNK) == (i[None, :] // CHUNK)
    return (same & (i[None, :] <= i[:, None])).astype(BF16)


def _hgrn_layer(x, w_in, norm_w, w_out, layer, lb, g, b, w_gate_up, w_down, ffn_layer, g2, b2, tb):
    seq = x.shape[0]
    nb = seq // tb
    head_scratch = pltpu.VMEM((HGRN_HEADS, tb, HGRN_DK), F32)
    spec0, spec1, spec2 = _pipeline_specs(tb, D_MODEL, nb, 3)
    vec = _const_spec((1, D_MODEL))
    return pl.pallas_call(
        functools.partial(_hgrn_kernel, tb=tb),
        out_shape=jax.ShapeDtypeStruct((seq, D_MODEL), F32),
        grid=(nb + 2,),
        in_specs=[spec0, spec1, _layer_spec(w_in.shape, layer), vec, _const_spec((1, HGRN_DV)),
                  _layer_spec(w_out.shape, layer), vec, vec, _const_spec((tb, tb)),
                  _layer_spec(w_gate_up.shape, ffn_layer), _layer_spec(w_down.shape, ffn_layer), vec, vec],
        out_specs=spec2,
        scratch_shapes=[head_scratch] * 6 + [pltpu.VMEM((HGRN_HEADS, HGRN_DV, HGRN_DK), F32),
                                             pltpu.VMEM((tb, D_MODEL), F32)],
        compiler_params=_params(),
        name="hgrn_layer",
    )(x, x, w_in, lb, norm_w, w_out, g, b, _chunk_tri(tb), w_gate_up, w_down, g2, b2)


def _mlstm_kernel(xf_ref, xb_ref, win_ref, wg_ref, conv_ref, gb_ref, nw_ref, wout_ref, g_ref, b_ref, tri_ref, sel_ref,
                  wgu_ref, wd_ref, g2_ref, b2_ref, o_ref,
                  pad_s, q_s, k_s, v_s, crow_s, gcol_s, wi_s, em_s, ws_s, wp_s, sigo_s, h_s, c_s, m_s, x1_s, *, tb):
    nc = tb // CHUNK
    halo = SUB

    @pl.when(pl.program_id(0) == 0)
    def _():
        pad_s[0:halo, :] = jnp.zeros((halo, 2 * MLSTM_QK_W), F32)
        c_s[...] = jnp.zeros_like(c_s)
        m_s[...] = jnp.zeros_like(m_s)
        for ref in (q_s, k_s, v_s, crow_s, gcol_s, wi_s, ws_s, wp_s, sigo_s, x1_s):
            ref[...] = jnp.zeros_like(ref)
        em_s[...] = jnp.ones_like(em_s)

    ffn = _FfnPieces(x1_s[...], wgu_ref, wd_ref)
    xb = xf_ref[...].astype(BF16)
    proj = _dot(xb, win_ref[...])
    v_new = proj[:, 2 * MLSTM_QK_W:2 * MLSTM_QK_W + D_MODEL]
    o_pre = proj[:, 2 * MLSTM_QK_W + D_MODEL:]
    pad_s[halo:halo + tb, :] = proj[:, :2 * MLSTM_QK_W]
    conv_w = conv_ref[...]
    qk = pad_s[pl.ds(halo - MLSTM_CONV + 1, tb), :] * conv_w[0:1, :]
    for j in range(1, MLSTM_CONV):
        qk = qk + pad_s[pl.ds(halo - MLSTM_CONV + 1 + j, tb), :] * conv_w[j:j + 1, :]
    pad_s[0:halo, :] = pad_s[tb:tb + halo, :]
    qk = qk * _sigmoid(qk)
    q_new = qk[:, :MLSTM_QK_W]
    k_new = qk[:, MLSTM_QK_W:] * (MLSTM_DQK ** -0.5)
    gates = _dot(xb, wg_ref[...]) + gb_ref[...]
    li = gates[:, :LANES]
    bcum = _chunk_cumsum(tri_ref[...], _log_sigmoid(gates[:, LANES:]))
    cq = li - bcum
    rin = lax.broadcasted_iota(jnp.int32, (tb, LANES), 0) & (CHUNK - 1)
    cm = cq
    k = 1
    while k < CHUNK:
        shifted = pltpu.roll(cm, k, 0)
        cm = jnp.maximum(cm, jnp.where(rin >= k, shifted, NEG_INF))
        k *= 2
    m_prev = m_s[...]
    g_parts, wi_parts, em_parts, ws_parts, wp_parts, crow_new = [], [], [], [], [], []
    for c in range(nc):
        sl = slice(c * CHUNK, (c + 1) * CHUNK)
        mx = jnp.maximum(m_prev, cm[(c + 1) * CHUNK - 1:(c + 1) * CHUNK])
        g_c = jnp.maximum(cm[sl], m_prev)
        g_parts.append(g_c)
        wi_parts.append(jnp.exp(m_prev - g_c))
        em_parts.append(jnp.exp(-(bcum[sl] + g_c)))
        ws_parts.append(jnp.exp(cq[sl] - mx))
        wp_parts.append(jnp.exp(m_prev - mx))
        m_prev = bcum[(c + 1) * CHUNK - 1:(c + 1) * CHUNK] + mx
        crow_new.append(cq[sl].T[0:SUB, :])
    m_s[...] = m_prev
    wp_parts.append(jnp.zeros((SUB - nc, LANES), F32))
    stack = jnp.concatenate(g_parts + wi_parts + em_parts + ws_parts + wp_parts, axis=0)
    hi = stack.astype(BF16)
    r1 = stack - hi.astype(F32)
    mid = r1.astype(BF16)
    lo = (r1 - mid.astype(F32)).astype(BF16)
    spread = _dot(jnp.concatenate([hi, mid, lo], axis=1), sel_ref[...])
    sig_o = _sigmoid(o_pre)

    lane_c = lax.broadcasted_iota(jnp.int32, (CHUNK, LANES), 1)
    rowi = lax.broadcasted_iota(jnp.int32, (CHUNK, CHUNK), 0)
    coli = lax.broadcasted_iota(jnp.int32, (CHUNK, CHUNK), 1)
    causal = rowi >= coli
    first_head = lane_c < MLSTM_DQK
    first_rows = lax.broadcasted_iota(jnp.int32, (2 * MLSTM_DQK, 2 * MLSTM_DV), 0) < MLSTM_DQK
    ones_blk = jnp.ones((CHUNK, LANES), BF16)
    nw_all = nw_ref[...]

    n_pairs = MLSTM_HEADS // 2
    n_units = 2 * nc * n_pairs
    unit = 0
    for c in range(nc):
        rows = slice(c * CHUNK, (c + 1) * CHUNK)
        crow = crow_s[c]
        wp_row = wp_s[c:c + 1, :]
        staged = []
        for p in range(n_pairs):
            lanes_p = slice(p * LANES, (p + 1) * LANES)
            cext = c_s[p]
            q_pair = q_s[rows, lanes_p]
            k_pair = k_s[rows, lanes_p]
            hsl = [slice((2 * p + j) * MLSTM_DV, (2 * p + j + 1) * MLSTM_DV) for j in range(2)]
            qm = jnp.concatenate([jnp.where(first_head, q_pair, 0.0), jnp.where(first_head, 0.0, q_pair)],
                                 axis=0).astype(BF16)
            v_ext = [jnp.concatenate([v_s[rows, hsl[j]].astype(BF16), ones_blk], axis=1) for j in range(2)]
            scores = _dot_nt(qm, k_pair.astype(BF16))
            inter = _dot(qm, cext.astype(BF16))
            kw = jnp.concatenate([jnp.where(first_head, k_pair * ws_s[rows, hsl[0]], 0.0),
                                  jnp.where(first_head, 0.0, k_pair * ws_s[rows, hsl[1]])], axis=0).astype(BF16)
            update = _dot_tn(kw, jnp.concatenate(v_ext, axis=0))
            w_prev = jnp.where(first_rows, jnp.tile(wp_row[:, hsl[0]], (1, 2)), jnp.tile(wp_row[:, hsl[1]], (1, 2)))
            c_s[p] = w_prev * cext + update
            staged.append((hsl, v_ext, scores, inter))
            unit += 1
            ffn.advance_to(unit / n_units)
        for p in range(n_pairs):
            hsl, v_ext, scores, inter = staged[p]
            for j in range(2):
                h = 2 * p + j
                hr = slice(j * CHUNK, (j + 1) * CHUNK)
                d_mat = jnp.exp(jnp.where(causal, crow[h:h + 1, :] - gcol_s[rows, h * MLSTM_DV:h * MLSTM_DV + CHUNK],
                                          NEG_INF))
                s = scores[hr] * d_mat
                both = _dot(s.astype(BF16), v_ext[j]) + jnp.tile(wi_s[rows, hsl[j]], (1, 2)) * inter[hr]
                h_s[rows, hsl[j]] = both[:, :MLSTM_DV] / jnp.maximum(jnp.abs(both[:, MLSTM_DV:]), em_s[rows, hsl[j]])
            unit += 1
            ffn.advance_to(unit / n_units)
    o_ref[...] = ffn.finish(g2_ref, b2_ref)

    normed = []
    for h in range(MLSTM_HEADS):
        hsl = slice(h * MLSTM_DV, (h + 1) * MLSTM_DV)
        hh = h_s[:, hsl]
        ms = jnp.mean(hh * hh, axis=-1, keepdims=True)
        normed.append((hh * lax.rsqrt(ms + HEAD_NORM_EPS) * nw_all[:, hsl] * sigo_s[:, hsl]).astype(BF16))
    y = _dot(jnp.concatenate(normed, axis=1), wout_ref[...])
    x1_new = _layernorm(ALPHA * xb_ref[...] + y, g_ref[...], b_ref[...])

    x1_s[...] = x1_new
    q_s[...] = q_new
    k_s[...] = k_new
    v_s[...] = v_new
    sigo_s[...] = sig_o
    for c in range(nc):
        crow_s[c] = crow_new[c]
    gcol_s[...] = spread[0:tb]
    wi_s[...] = spread[tb:2 * tb]
    em_s[...] = spread[2 * tb:3 * tb]
    ws_s[...] = spread[3 * tb:4 * tb]
    wp_s[...] = spread[4 * tb:4 * tb + SUB]


MLSTM_MAIN_W = 2 * MLSTM_QK_W + 2 * D_MODEL


def _mlstm_gate_weights(w_in, gate_b):
    pad = LANES - MLSTM_HEADS
    w_i = jnp.pad(w_in[:, MLSTM_MAIN_W:MLSTM_MAIN_W + MLSTM_HEADS], ((0, 0), (0, pad)))
    w_f = jnp.pad(w_in[:, MLSTM_MAIN_W + MLSTM_HEADS:], ((0, 0), (0, pad)))
    b_i = jnp.pad(gate_b[:MLSTM_HEADS], (0, pad))
    b_f = jnp.pad(gate_b[MLSTM_HEADS:], (0, pad))
    return jnp.concatenate([w_i, w_f], axis=1).astype(BF16), jnp.concatenate([b_i, b_f]).reshape(1, 2 * LANES)


def _lane_spread_matrix():
    src = jnp.arange(LANES)[:, None]
    dst = jnp.arange(D_MODEL)[None, :] // LANES
    return jnp.tile((src == dst).astype(BF16), (3, 1))


def _mlstm_layer(x, w_in, w_gate, conv_w, gate_b, norm_w, w_out, layer, g, b,
                 w_gate_up, w_down, ffn_layer, g2, b2, tb):
    seq = x.shape[0]
    nb = seq // tb
    assert tb // CHUNK <= SUB
    wide = pltpu.VMEM((tb, D_MODEL), F32)
    sel = _lane_spread_matrix()
    spec0, spec1, spec2 = _pipeline_specs(tb, D_MODEL, nb, 3)
    vec = _const_spec((1, D_MODEL))
    return pl.pallas_call(
        functools.partial(_mlstm_kernel, tb=tb),
        out_shape=jax.ShapeDtypeStruct((seq, D_MODEL), F32),
        grid=(nb + 2,),
        in_specs=[spec0, spec1, _layer_spec(w_in.shape, layer, MLSTM_MAIN_W), _const_spec(w_gate.shape),
                  _const_spec(conv_w.shape), _const_spec((1, 2 * LANES)), vec,
                  _layer_spec(w_out.shape, layer), vec, vec, _const_spec((tb, tb)), _const_spec(sel.shape),
                  _layer_spec(w_gate_up.shape, ffn_layer), _layer_spec(w_down.shape, ffn_layer), vec, vec],
        out_specs=spec2,
        scratch_shapes=[pltpu.VMEM((tb + SUB, 2 * MLSTM_QK_W), F32),
                        pltpu.VMEM((tb, MLSTM_QK_W), F32), pltpu.VMEM((tb, MLSTM_QK_W), F32),
                        wide, pltpu.VMEM((tb // CHUNK, SUB, CHUNK), F32), wide, wide, wide, wide,
                        pltpu.VMEM((SUB, D_MODEL), F32), wide, wide,
                        pltpu.VMEM((MLSTM_HEADS // 2, 2 * MLSTM_DQK, 2 * MLSTM_DV), F32),
                        pltpu.VMEM((1, LANES), F32), wide],
        compiler_params=_params(),
        name="mlstm_layer",
    )(x, x, w_in, w_gate, conv_w, gate_b, norm_w, w_out, g, b, _chunk_tri(tb), sel, w_gate_up, w_down, g2, b2)


S5_LANE_GROUPS = LANES // S5_GROUP_CH
S5_BLOCKS = D_MODEL // LANES
S5_BLOCK_STATE = S5_LANE_GROUPS * S5_STATE


def _s5_kernel(x_ref, win_ref, bre_ref, bim_ref, cre_ref, cim_ref, are_ref, aim_ref, d_ref, wout_ref,
               g_ref, b_ref, wgu_ref, wd_ref, g2_ref, b2_ref, o_ref, hr_s, hi_s, y_s, x1_s, *, tb):
    @pl.when(pl.program_id(0) == 0)
    def _():
        hr_s[...] = jnp.zeros_like(hr_s)
        hi_s[...] = jnp.zeros_like(hi_s)
        x1_s[...] = jnp.zeros_like(x1_s)

    ffn = _FfnPieces(x1_s[...], wgu_ref, wd_ref)
    x = x_ref[...]
    u = _dot(x.astype(BF16), win_ref[...])
    ub = u.astype(BF16)
    nt = tb // SUB
    row8 = lax.broadcasted_iota(jnp.int32, (SUB, S5_BLOCK_STATE), 0)

    def tile_scan(zr, zi, mults):
        for i, (mr, mi) in enumerate(mults):
            rr = pltpu.roll(zr, 1 << i, 1)
            ri = pltpu.roll(zi, 1 << i, 1)
            zr, zi = zr + (mr * rr - mi * ri), zi + (mr * ri + mi * rr)
        return zr, zi

    for j in range(S5_BLOCKS):
        uj = ub[:, j * LANES:(j + 1) * LANES]
        st = slice(j * S5_BLOCK_STATE, (j + 1) * S5_BLOCK_STATE)
        ar = are_ref[:, st]
        ai = aim_ref[:, st]
        mults = []
        pr, pi = ar, ai
        for i in range(3):
            keep = row8 >= (1 << i)
            mults.append((jnp.where(keep, pr, 0.0), jnp.where(keep, pi, 0.0)))
            pr, pi = pr * pr - pi * pi, 2.0 * pr * pi
        pwr, pwi = tile_scan(jnp.where(row8 == 0, ar, 0.0)[None], jnp.where(row8 == 0, ai, 0.0)[None], mults)
        pwr, pwi = pwr[0], pwi[0]
        sr = _dot(uj, bre_ref[j]).reshape(nt, SUB, S5_BLOCK_STATE)
        si = _dot(uj, bim_ref[j]).reshape(nt, SUB, S5_BLOCK_STATE)
        ffn.advance_to((j + 1) / S5_BLOCKS)
        sr, si = tile_scan(sr, si, mults)
        cr = hr_s[:, st]
        ci = hi_s[:, st]
        tiles_r, tiles_i = [], []
        for n in range(nt):
            tr = sr[n] + (pwr * cr - pwi * ci)
            ti = si[n] + (pwr * ci + pwi * cr)
            tiles_r.append(tr)
            tiles_i.append(ti)
            cr, ci = tr[SUB - 1:SUB, :], ti[SUB - 1:SUB, :]
        hr_s[:, st] = cr
        hi_s[:, st] = ci
        hr = jnp.concatenate(tiles_r, axis=0)
        hi = jnp.concatenate(tiles_i, axis=0)
        y_s[:, j * LANES:(j + 1) * LANES] = _dot(hr.astype(BF16), cre_ref[j]) - _dot(hi.astype(BF16), cim_ref[j])
    o_ref[...] = ffn.finish(g2_ref, b2_ref)

    y = jax.nn.gelu(y_s[...] + d_ref[...] * u)
    z = _dot(y.astype(BF16), wout_ref[...])
    mix = z[:, :D_MODEL] * _sigmoid(z[:, D_MODEL:])
    x1_s[...] = _layernorm(ALPHA * x + mix, g_ref[...], b_ref[...])


def _s5_block_diag(t):
    a, b = t.shape[1], t.shape[2]
    t = t.reshape(S5_BLOCKS, S5_LANE_GROUPS, a, b)
    eye = jnp.eye(S5_LANE_GROUPS, dtype=t.dtype)
    out = jnp.einsum("jgab,gh->jgahb", t, eye)
    return out.reshape(S5_BLOCKS, S5_LANE_GROUPS * a, S5_LANE_GROUPS * b)


def _s5_layer(x, w_in, a_re, a_im, log_dt, b_re, b_im, c_re, c_im, d_skip, w_out, layer, g, b,
              w_gate_up, w_down, ffn_layer, g2, b2, tb):
    seq = x.shape[0]
    nb = seq // tb
    dt = jnp.exp(log_dt)[:, None]
    mag = jnp.exp(a_re * dt)
    abar_re = mag * jnp.cos(a_im * dt)
    abar_im = mag * jnp.sin(a_im * dt)
    nr = abar_re - 1.0
    ni = abar_im
    den = a_re * a_re + a_im * a_im
    coef_re = (nr * a_re + ni * a_im) / den
    coef_im = (ni * a_re - nr * a_im) / den
    bbar_re = coef_re[..., None] * b_re - coef_im[..., None] * b_im
    bbar_im = coef_re[..., None] * b_im + coef_im[..., None] * b_re
    bre = _s5_block_diag(jnp.swapaxes(bbar_re, 1, 2)).astype(BF16)
    bim = _s5_block_diag(jnp.swapaxes(bbar_im, 1, 2)).astype(BF16)
    cre = _s5_block_diag(jnp.swapaxes(c_re, 1, 2)).astype(BF16)
    cim = _s5_block_diag(jnp.swapaxes(c_im, 1, 2)).astype(BF16)
    n_state = S5_GROUPS * S5_STATE
    spec0, spec1 = _pipeline_specs(tb, D_MODEL, nb, 2)
    vec = _const_spec((1, D_MODEL))
    return pl.pallas_call(
        functools.partial(_s5_kernel, tb=tb),
        out_shape=jax.ShapeDtypeStruct((seq, D_MODEL), F32),
        grid=(nb + 1,),
        in_specs=[spec0, _layer_spec(w_in.shape, layer), _const_spec(bre.shape),
                  _const_spec(bim.shape), _const_spec(cre.shape), _const_spec(cim.shape), _const_spec((1, n_state)),
                  _const_spec((1, n_state)), vec, _layer_spec(w_out.shape, layer), vec, vec,
                  _layer_spec(w_gate_up.shape, ffn_layer), _layer_spec(w_down.shape, ffn_layer), vec, vec],
        out_specs=spec1,
        scratch_shapes=[pltpu.VMEM((1, n_state), F32), pltpu.VMEM((1, n_state), F32),
                        pltpu.VMEM((tb, D_MODEL), F32), pltpu.VMEM((tb, D_MODEL), F32)],
        compiler_params=_params(),
        name="s5_layer",
    )(x, w_in, bre, bim, cre, cim, abar_re.reshape(1, n_state), abar_im.reshape(1, n_state),
      d_skip.reshape(1, D_MODEL), w_out, g, b, w_gate_up, w_down, g2, b2)


def _hgrn_lower_bounds(lb_logits):
    p = jax.nn.softmax(lb_logits.astype(F32), axis=0)
    c = jnp.cumsum(p, axis=0)
    return c - c[0:1]


def kernel(x, hgrn_w_in, hgrn_norm_w, hgrn_w_out, hgrn_lb_logits, mlstm_w_in, mlstm_conv_w, mlstm_gate_b, mlstm_norm_w, mlstm_w_out, s5_w_in, s5_a_re, s5_a_im, s5_log_dt, s5_b_re, s5_b_im, s5_c_re, s5_c_im, s5_d, s5_w_out, ffn_w_gate_up, ffn_w_down, ln_g, ln_b):
    bsz, seq, _ = x.shape
    tb = math.gcd(seq, ROW_BLOCK)
    assert tb % CHUNK == 0, "sequence length must be a multiple of the recurrence chunk"
    lb_all = _hgrn_lower_bounds(hgrn_lb_logits)
    row = lambda t: t.reshape(1, -1)
    (hgrn_w_in, hgrn_w_out, mlstm_w_in_b, mlstm_w_out, s5_w_in, s5_w_out, ffn_w_gate_up, ffn_w_down) = (
        w.astype(BF16) for w in (hgrn_w_in, hgrn_w_out, mlstm_w_in, mlstm_w_out, s5_w_in, s5_w_out,
                                 ffn_w_gate_up, ffn_w_down))
    outs = []
    for bi in range(bsz):
        h = x[bi]
        for i in range(DEPTH):
            kind = i % N_MIXERS
            j = i // N_MIXERS
            g0, b0 = row(ln_g[i, 0]), row(ln_b[i, 0])
            ffn_args = (ffn_w_gate_up, ffn_w_down, i, row(ln_g[i, 1]), row(ln_b[i, 1]), tb)
            if kind == 0:
                h = _hgrn_layer(h, hgrn_w_in, row(hgrn_norm_w[j]), hgrn_w_out, j, row(lb_all[i]), g0, b0, *ffn_args)
            elif kind == 1:
                w_gate, gate_b = _mlstm_gate_weights(mlstm_w_in[j], mlstm_gate_b[j])
                h = _mlstm_layer(h, mlstm_w_in_b, w_gate, mlstm_conv_w[j], gate_b, row(mlstm_norm_w[j]),
                                 mlstm_w_out, j, g0, b0, *ffn_args)
            else:
                h = _s5_layer(h, s5_w_in, s5_a_re[j], s5_a_im[j], s5_log_dt[j], s5_b_re[j], s5_b_im[j],
                              s5_c_re[j], s5_c_im[j], s5_d[j], s5_w_out, j, g0, b0, *ffn_args)
        outs.append(h)
    return jnp.stack(outs, axis=0)
```

```python
import functools
import math

import jax
import jax.numpy as jnp
from jax import lax
from jax.experimental import pallas as pl
from jax.experimental.pallas import tpu as pltpu

F32 = jnp.float32
BF16 = jnp.bfloat16

D_MODEL = 1024
DEPTH = 4
N_MIXERS = 3
ALPHA = (2.0 * DEPTH) ** 0.25
LN_EPS = 1e-5
HEAD_NORM_EPS = 1e-6

HGRN_HEADS = 8
HGRN_DK = 128
HGRN_DV = 128

MLSTM_HEADS = 8
MLSTM_DV = 128
MLSTM_DQK = 64
MLSTM_CONV = 4
MLSTM_QK_W = MLSTM_HEADS * MLSTM_DQK

S5_GROUP_CH = 16
S5_GROUPS = D_MODEL // S5_GROUP_CH
S5_STATE = 64

FFN_HIDDEN = -(-8 * D_MODEL // (3 * 256)) * 256

HGRN_HEAD_GROUP = 8
CHUNK = 64
SUB = 8
LANES = 128
ROW_BLOCK = 256
VMEM_LIMIT_BYTES = 60 * 1024 * 1024

NEG_INF = float("-inf")


def _dot(a, b):
    return jnp.dot(a, b, preferred_element_type=F32)


def _dot_nt(a, b):
    return lax.dot_general(a, b, (((1,), (1,)), ((), ())), preferred_element_type=F32)


def _dot_tn(a, b):
    return lax.dot_general(a, b, (((0,), (0,)), ((), ())), preferred_element_type=F32)


def _layernorm(y, g, b):
    mu = jnp.mean(y, axis=-1, keepdims=True)
    yc = y - mu
    var = jnp.mean(yc * yc, axis=-1, keepdims=True)
    return yc * lax.rsqrt(var + LN_EPS) * g + b


def _log1p_exp_neg(z):
    return jnp.log(1.0 + jnp.exp(-z))


def _log_sigmoid(z):
    return jnp.minimum(z, 0.0) - _log1p_exp_neg(jnp.abs(z))


LOG2E = 1.4426950408889634


def _sigmoid(z):
    return 1.0 / (1.0 + jnp.exp2(z * -LOG2E))


def _chunk_cumsum(tri, z):
    hi = z.astype(BF16)
    r1 = z - hi.astype(F32)
    mid = r1.astype(BF16)
    lo = (r1 - mid.astype(F32)).astype(BF16)
    return _dot(tri, hi) + _dot(tri, mid) + _dot(tri, lo)


def _const_spec(shape):
    nd = len(shape)
    return pl.BlockSpec(shape, lambda i: (0,) * nd, pipeline_mode=pl.Buffered(1))


def _layer_spec(stacked_shape, layer, cols=None):
    _, rows, width = stacked_shape
    return pl.BlockSpec((None, rows, cols or width), lambda i: (layer, 0, 0), pipeline_mode=pl.Buffered(1))


def _pipeline_specs(tb, width, nb, stages):
    return [pl.BlockSpec((tb, width), functools.partial(lambda i, s: (jnp.clip(i - s, 0, nb - 1), 0), s=s))
            for s in range(stages)]


def _params():
    return pltpu.CompilerParams(dimension_semantics=("arbitrary",), vmem_limit_bytes=VMEM_LIMIT_BYTES)


FFN_PIECE = 256


class _FfnPieces:
    def __init__(self, x, wgu_ref, wd_ref):
        self.x = x
        self.xb = x.astype(BF16)
        self.wgu_ref = wgu_ref
        self.wd_ref = wd_ref
        self.n_up = FFN_HIDDEN // FFN_PIECE
        self.n_down = D_MODEL // FFN_PIECE
        self.done = 0
        self.act = []
        self.out = []

    def advance_to(self, fraction):
        total = self.n_up + self.n_down
        target = min(total, int(fraction * total + 1e-9))
        while self.done < target:
            if self.done < self.n_up:
                lo = self.done * FFN_PIECE
                gate = _dot(self.xb, self.wgu_ref[:, lo:lo + FFN_PIECE])
                up = _dot(self.xb, self.wgu_ref[:, FFN_HIDDEN + lo:FFN_HIDDEN + lo + FFN_PIECE])
                self.act.append((gate * _sigmoid(gate) * up).astype(BF16))
            else:
                if len(self.act) > 1:
                    self.act = [jnp.concatenate(self.act, axis=1)]
                lo = (self.done - self.n_up) * FFN_PIECE
                self.out.append(_dot(self.act[0], self.wd_ref[:, lo:lo + FFN_PIECE]))
            self.done += 1

    def finish(self, g_ref, b_ref):
        self.advance_to(1.0)
        return _layernorm(ALPHA * self.x + jnp.concatenate(self.out, axis=1), g_ref[...], b_ref[...])


def _hgrn_kernel(xf_ref, xb_ref, win_ref, lb_ref, nw_ref, wout_ref, g_ref, b_ref, tri_ref,
                 wgu_ref, wd_ref, g2_ref, b2_ref, o_ref,
                 q_s, k_s, v_s, b_s, gate_s, o_s, state_s, x1_s, *, tb):
    nc = tb // CHUNK

    @pl.when(pl.program_id(0) == 0)
    def _():
        state_s[...] = jnp.zeros_like(state_s)
        for ref in (q_s, k_s, v_s, b_s, gate_s, x1_s):
            ref[...] = jnp.zeros_like(ref)

    ffn = _FfnPieces(x1_s[...], wgu_ref, wd_ref)

    proj = _dot(xf_ref[...].astype(BF16), win_ref[...])
    q = proj[:, 0:D_MODEL]
    f = proj[:, D_MODEL:2 * D_MODEL]
    v = proj[:, 2 * D_MODEL:3 * D_MODEL]
    gt = proj[:, 3 * D_MODEL:4 * D_MODEL]
    lb = lb_ref[...]
    a1 = jnp.log(lb)
    a2 = jnp.log1p(-lb) + _log_sigmoid(f)
    log_f = jnp.maximum(a1, a2) + _log1p_exp_neg(jnp.abs(a1 - a2))
    k = (1.0 - lb) * _sigmoid(-f)
    q = q * _sigmoid(q)
    gate = gt * _sigmoid(gt)
    bcum = _chunk_cumsum(tri_ref[...], log_f) * LOG2E

    row = lax.broadcasted_iota(jnp.int32, (CHUNK, LANES), 0)
    rowi = lax.broadcasted_iota(jnp.int32, (CHUNK, CHUNK), 0)
    coli = lax.broadcasted_iota(jnp.int32, (CHUNK, CHUNK), 1)
    row8 = lax.broadcasted_iota(jnp.int32, (SUB, CHUNK), 0)
    lane8 = lax.broadcasted_iota(jnp.int32, (SUB, CHUNK), 1)
    place = [[((lane8 == blk * SUB + s) & (row8 >= s)).astype(F32) for s in range(SUB)]
             for blk in range(CHUNK // SUB)]
    same_block = {}
    m = SUB
    while m < CHUNK:
        shift = int(math.log2(2 * m))
        same_block[m] = ((rowi >> shift) == (coli >> shift)).astype(F32)
        m *= 2
    nw = nw_ref[...]

    def head_scores(h, rows):
        qc = q_s[h, rows, :]
        kc = k_s[h, rows, :]
        bc = b_s[h, rows, :]
        att = jnp.zeros((CHUNK, CHUNK), F32)
        m = SUB
        while m < CHUNK:
            ref = jnp.concatenate(
                [jnp.broadcast_to(bc[blk * 2 * m + m - 1:blk * 2 * m + m, :], (2 * m, LANES))
                 for blk in range(CHUNK // (2 * m))], axis=0)
            upper = (row & m) != 0
            qh = qc * jnp.exp2(jnp.where(upper, bc - ref, NEG_INF))
            kh = kc * jnp.exp2(jnp.where(upper, NEG_INF, ref - bc))
            pair = _dot_nt(qh.astype(BF16), kh.astype(BF16))
            att = att + pair * same_block[m]
            m *= 2
        tiles = []
        for blk in range(CHUNK // SUB):
            sl = slice(blk * SUB, (blk + 1) * SUB)
            qi, bi = qc[sl], bc[sl]
            tile = jnp.zeros((SUB, CHUNK), F32)
            for s in range(SUB):
                r = rows.start + blk * SUB + s
                ks = jnp.broadcast_to(k_s[h, r:r + 1, :], (SUB, LANES))
                bs = jnp.broadcast_to(b_s[h, r:r + 1, :], (SUB, LANES))
                decay = jnp.exp2(jnp.minimum(bi - bs, 0.0))
                col = jnp.sum(qi * (ks * decay), axis=1, keepdims=True)
                tile = tile + col * place[blk][s]
            tiles.append(tile)
        return (att + jnp.concatenate(tiles, axis=0)).astype(BF16)

    def head_output(h, rows, att):
        st = state_s[h]
        qc = q_s[h, rows, :]
        kc = k_s[h, rows, :]
        vcb = v_s[h, rows, :].astype(BF16)
        bc = b_s[h, rows, :]
        b_last = bc[CHUNK - 1:CHUNK, :]
        o_s[h, rows, :] = _dot(att, vcb) + _dot_nt((qc * jnp.exp2(bc)).astype(BF16), st.astype(BF16))
        k_dec = kc * jnp.exp2(b_last - bc)
        state_s[h] = st * jnp.exp2(b_last) + _dot_tn(vcb, k_dec.astype(BF16))

    n_units = 2 * nc * HGRN_HEADS
    unit = 0
    for c in range(nc):
        rows = slice(c * CHUNK, (c + 1) * CHUNK)
        atts = []
        for h in range(HGRN_HEADS):
            atts.append(head_scores(h, rows))
            unit += 1
            ffn.advance_to(unit / n_units)
        for h in range(HGRN_HEADS):
            head_output(h, rows, atts[h])
            unit += 1
            ffn.advance_to(unit / n_units)
    o_ref[...] = ffn.finish(g2_ref, b2_ref)

    normed = []
    for h in range(HGRN_HEADS):
        o = o_s[h]
        ms = jnp.mean(o * o, axis=-1, keepdims=True)
        normed.append((o * lax.rsqrt(ms + HEAD_NORM_EPS) * nw * gate_s[h]).astype(BF16))
    y = _dot(jnp.concatenate(normed, axis=1), wout_ref[...])
    x1_new = _layernorm(ALPHA * xb_ref[...] + y, g_ref[...], b_ref[...])

    x1_s[...] = x1_new
    for h in range(HGRN_HEADS):
        sl = slice(h * HGRN_DK, (h + 1) * HGRN_DK)
        q_s[h] = q[:, sl]
        k_s[h] = k[:, sl]
        v_s[h] = v[:, sl]
        b_s[h] = bcum[:, sl]
        gate_s[h] = gate[:, sl]


def _chunk_tri(tb):
    i = jnp.arange(tb)
    same = (i[:, None] // CHUNK) == (i[None, :] // CHUNK)
    return (same & (i[None, :] <= i[:, None])).astype(BF16)


def _hgrn_layer(x, w_in, norm_w, w_out, layer, lb, g, b, w_gate_up, w_down, ffn_layer, g2, b2, tb):
    seq = x.shape[0]
    nb = seq // tb
    head_scratch = pltpu.VMEM((HGRN_HEADS, tb, HGRN_DK), F32)
    spec0, spec1, spec2 = _pipeline_specs(tb, D_MODEL, nb, 3)
    vec = _const_spec((1, D_MODEL))
    return pl.pallas_call(
        functools.partial(_hgrn_kernel, tb=tb),
        out_shape=jax.ShapeDtypeStruct((seq, D_MODEL), F32),
        grid=(nb + 2,),
        in_specs=[spec0, spec1, _layer_spec(w_in.shape, layer), vec, _const_spec((1, HGRN_DV)),
                  _layer_spec(w_out.shape, layer), vec, vec, _const_spec((tb, tb)),
                  _layer_spec(w_gate_up.shape, ffn_layer), _layer_spec(w_down.shape, ffn_layer), vec, vec],
        out_specs=spec2,
        scratch_shapes=[head_scratch] * 6 + [pltpu.VMEM((HGRN_HEADS, HGRN_DV, HGRN_DK), F32),
                                             pltpu.VMEM((tb, D_MODEL), F32)],
        compiler_params=_params(),
        name="hgrn_layer",
    )(x, x, w_in, lb, norm_w, w_out, g, b, _chunk_tri(tb), w_gate_up, w_down, g2, b2)


def _mlstm_kernel(xf_ref, xb_ref, win_ref, wg_ref, conv_ref, gb_ref, nw_ref, wout_ref, g_ref, b_ref, tri_ref, sel_ref,
                  wgu_ref, wd_ref, g2_ref, b2_ref, o_ref,
                  pad_s, q_s, k_s, v_s, crow_s, gcol_s, wi_s, em_s, ws_s, wp_s, sigo_s, h_s, c_s, m_s, x1_s, *, tb):
    nc = tb // CHUNK
    halo = SUB

    @pl.when(pl.program_id(0) == 0)
    def _():
        pad_s[0:halo, :] = jnp.zeros((halo, 2 * MLSTM_QK_W), F32)
        c_s[...] = jnp.zeros_like(c_s)
        m_s[...] = jnp.zeros_like(m_s)
        for ref in (q_s, k_s, v_s, crow_s, gcol_s, wi_s, ws_s, wp_s, sigo_s, x1_s):
            ref[...] = jnp.zeros_like(ref)
        em_s[...] = jnp.ones_like(em_s)

    ffn = _FfnPieces(x1_s[...], wgu_ref, wd_ref)
    xb = xf_ref[...].astype(BF16)
    proj = _dot(xb, win_ref[...])
    v_new = proj[:, 2 * MLSTM_QK_W:2 * MLSTM_QK_W + D_MODEL]
    o_pre = proj[:, 2 * MLSTM_QK_W + D_MODEL:]
    pad_s[halo:halo + tb, :] = proj[:, :2 * MLSTM_QK_W]
    conv_w = conv_ref[...]
    qk = pad_s[pl.ds(halo - MLSTM_CONV + 1, tb), :] * conv_w[0:1, :]
    for j in range(1, MLSTM_CONV):
        qk = qk + pad_s[pl.ds(halo - MLSTM_CONV + 1 + j, tb), :] * conv_w[j:j + 1, :]
    pad_s[0:halo, :] = pad_s[tb:tb + halo, :]
    qk = qk * _sigmoid(qk)
    q_new = qk[:, :MLSTM_QK_W]
    k_new = qk[:, MLSTM_QK_W:] * (MLSTM_DQK ** -0.5)
    gates = _dot(xb, wg_ref[...]) + gb_ref[...]
    li = gates[:, :LANES]
    bcum = _chunk_cumsum(tri_ref[...], _log_sigmoid(gates[:, LANES:]))
    cq = li - bcum
    rin = lax.broadcasted_iota(jnp.int32, (tb, LANES), 0) & (CHUNK - 1)
    cm = cq
    k = 1
    while k < CHUNK:
        shifted = pltpu.roll(cm, k, 0)
        cm = jnp.maximum(cm, jnp.where(rin >= k, shifted, NEG_INF))
        k *= 2
    m_prev = m_s[...]
    g_parts, wi_parts, em_parts, ws_parts, wp_parts, crow_new = [], [], [], [], [], []
    for c in range(nc):
        sl = slice(c * CHUNK, (c + 1) * CHUNK)
        mx = jnp.maximum(m_prev, cm[(c + 1) * CHUNK - 1:(c + 1) * CHUNK])
        g_c = jnp.maximum(cm[sl], m_prev)
        g_parts.append(g_c)
        wi_parts.append(jnp.exp(m_prev - g_c))
        em_parts.append(jnp.exp(-(bcum[sl] + g_c)))
        ws_parts.append(jnp.exp(cq[sl] - mx))
        wp_parts.append(jnp.exp(m_prev - mx))
        m_prev = bcum[(c + 1) * CHUNK - 1:(c + 1) * CHUNK] + mx
        crow_new.append(cq[sl].T[0:SUB, :])
    m_s[...] = m_prev
    wp_parts.append(jnp.zeros((SUB - nc, LANES), F32))
    def spread_lanes(tile, terms):
        parts, rest = [], tile
        for _ in range(terms):
            part = rest.astype(BF16)
            parts.append(part)
            rest = rest - part.astype(F32)
        return _dot(jnp.concatenate(parts, axis=1), sel_ref[0:terms * LANES, :])

    g_spread = spread_lanes(jnp.concatenate(g_parts, axis=0), 3)
    w_spread = spread_lanes(jnp.concatenate(wi_parts + em_parts + ws_parts + wp_parts, axis=0), 2)
    sig_o = _sigmoid(o_pre)

    lane_c = lax.broadcasted_iota(jnp.int32, (CHUNK, LANES), 1)
    rowi = lax.broadcasted_iota(jnp.int32, (CHUNK, CHUNK), 0)
    coli = lax.broadcasted_iota(jnp.int32, (CHUNK, CHUNK), 1)
    causal = rowi >= coli
    first_head = lane_c < MLSTM_DQK
    first_rows = lax.broadcasted_iota(jnp.int32, (2 * MLSTM_DQK, 2 * MLSTM_DV), 0) < MLSTM_DQK
    ones_blk = jnp.ones((CHUNK, LANES), BF16)
    nw_all = nw_ref[...]

    n_pairs = MLSTM_HEADS // 2
    n_units = 2 * nc * n_pairs
    unit = 0
    for c in range(nc):
        rows = slice(c * CHUNK, (c + 1) * CHUNK)
        crow = crow_s[c]
        wp_row = wp_s[c:c + 1, :]
        staged = []
        for p in range(n_pairs):
            lanes_p = slice(p * LANES, (p + 1) * LANES)
            cext = c_s[p]
            q_pair = q_s[rows, lanes_p]
            k_pair = k_s[rows, lanes_p]
            hsl = [slice((2 * p + j) * MLSTM_DV, (2 * p + j + 1) * MLSTM_DV) for j in range(2)]
            qm = jnp.concatenate([jnp.where(first_head, q_pair, 0.0), jnp.where(first_head, 0.0, q_pair)],
                                 axis=0).astype(BF16)
            v_ext = [jnp.concatenate([v_s[rows, hsl[j]].astype(BF16), ones_blk], axis=1) for j in range(2)]
            scores = _dot_nt(qm, k_pair.astype(BF16))
            inter = _dot(qm, cext.astype(BF16))
            kw = jnp.concatenate([jnp.where(first_head, k_pair * ws_s[rows, hsl[0]], 0.0),
                                  jnp.where(first_head, 0.0, k_pair * ws_s[rows, hsl[1]])], axis=0).astype(BF16)
            update = _dot_tn(kw, jnp.concatenate(v_ext, axis=0))
            w_prev = jnp.where(first_rows, jnp.tile(wp_row[:, hsl[0]], (1, 2)), jnp.tile(wp_row[:, hsl[1]], (1, 2)))
            c_s[p] = w_prev * cext + update
            staged.append((hsl, v_ext, scores, inter))
            unit += 1
            ffn.advance_to(unit / n_units)
        for p in range(n_pairs):
            hsl, v_ext, scores, inter = staged[p]
            for j in range(2):
                h = 2 * p + j
                hr = slice(j * CHUNK, (j + 1) * CHUNK)
                d_mat = jnp.exp(jnp.where(causal, crow[h:h + 1, :] - gcol_s[rows, h * MLSTM_DV:h * MLSTM_DV + CHUNK],
                                          NEG_INF))
                s = scores[hr] * d_mat
                both = _dot(s.astype(BF16), v_ext[j]) + jnp.tile(wi_s[rows, hsl[j]], (1, 2)) * inter[hr]
                h_s[rows, hsl[j]] = both[:, :MLSTM_DV] / jnp.maximum(jnp.abs(both[:, MLSTM_DV:]), em_s[rows, hsl[j]])
            unit += 1
            ffn.advance_to(unit / n_units)
    o_ref[...] = ffn.finish(g2_ref, b2_ref)

    normed = []
    for h in range(MLSTM_HEADS):
        hsl = slice(h * MLSTM_DV, (h + 1) * MLSTM_DV)
        hh = h_s[:, hsl]
        ms = jnp.mean(hh * hh, axis=-1, keepdims=True)
        normed.append((hh * lax.rsqrt(ms + HEAD_NORM_EPS) * nw_all[:, hsl] * sigo_s[:, hsl]).astype(BF16))
    y = _dot(jnp.concatenate(normed, axis=1), wout_ref[...])
    x1_new = _layernorm(ALPHA * xb_ref[...] + y, g_ref[...], b_ref[...])

    x1_s[...] = x1_new
    q_s[...] = q_new
    k_s[...] = k_new
    v_s[...] = v_new
    sigo_s[...] = sig_o
    for c in range(nc):
        crow_s[c] = crow_new[c]
    gcol_s[...] = g_spread
    wi_s[...] = w_spread[0:tb]
    em_s[...] = w_spread[tb:2 * tb]
    ws_s[...] = w_spread[2 * tb:3 * tb]
    wp_s[...] = w_spread[3 * tb:3 * tb + SUB]


MLSTM_MAIN_W = 2 * MLSTM_QK_W + 2 * D_MODEL


def _mlstm_gate_weights(w_in, gate_b):
    pad = LANES - MLSTM_HEADS
    w_i = jnp.pad(w_in[:, MLSTM_MAIN_W:MLSTM_MAIN_W + MLSTM_HEADS], ((0, 0), (0, pad)))
    w_f = jnp.pad(w_in[:, MLSTM_MAIN_W + MLSTM_HEADS:], ((0, 0), (0, pad)))
    b_i = jnp.pad(gate_b[:MLSTM_HEADS], (0, pad))
    b_f = jnp.pad(gate_b[MLSTM_HEADS:], (0, pad))
    return jnp.concatenate([w_i, w_f], axis=1).astype(BF16), jnp.concatenate([b_i, b_f]).reshape(1, 2 * LANES)


def _lane_spread_matrix():
    src = jnp.arange(LANES)[:, None]
    dst = jnp.arange(D_MODEL)[None, :] // LANES
    return jnp.tile((src == dst).astype(BF16), (3, 1))


def _mlstm_layer(x, w_in, w_gate, conv_w, gate_b, norm_w, w_out, layer, g, b,
                 w_gate_up, w_down, ffn_layer, g2, b2, tb):
    seq = x.shape[0]
    nb = seq // tb
    assert tb // CHUNK <= SUB
    wide = pltpu.VMEM((tb, D_MODEL), F32)
    sel = _lane_spread_matrix()
    spec0, spec1, spec2 = _pipeline_specs(tb, D_MODEL, nb, 3)
    vec = _const_spec((1, D_MODEL))
    return pl.pallas_call(
        functools.partial(_mlstm_kernel, tb=tb),
        out_shape=jax.ShapeDtypeStruct((seq, D_MODEL), F32),
        grid=(nb + 2,),
        in_specs=[spec0, spec1, _layer_spec(w_in.shape, layer, MLSTM_MAIN_W), _const_spec(w_gate.shape),
                  _const_spec(conv_w.shape), _const_spec((1, 2 * LANES)), vec,
                  _layer_spec(w_out.shape, layer), vec, vec, _const_spec((tb, tb)), _const_spec(sel.shape),
                  _layer_spec(w_gate_up.shape, ffn_layer), _layer_spec(w_down.shape, ffn_layer), vec, vec],
        out_specs=spec2,
        scratch_shapes=[pltpu.VMEM((tb + SUB, 2 * MLSTM_QK_W), F32),
                        pltpu.VMEM((tb, MLSTM_QK_W), F32), pltpu.VMEM((tb, MLSTM_QK_W), F32),
                        wide, pltpu.VMEM((tb // CHUNK, SUB, CHUNK), F32), wide, wide, wide, wide,
                        pltpu.VMEM((SUB, D_MODEL), F32), wide, wide,
                        pltpu.VMEM((MLSTM_HEADS // 2, 2 * MLSTM_DQK, 2 * MLSTM_DV), F32),
                        pltpu.VMEM((1, LANES), F32), wide],
        compiler_params=_params(),
        name="mlstm_layer",
    )(x, x, w_in, w_gate, conv_w, gate_b, norm_w, w_out, g, b, _chunk_tri(tb), sel, w_gate_up, w_down, g2, b2)


S5_LANE_GROUPS = LANES // S5_GROUP_CH
S5_BLOCKS = D_MODEL // LANES
S5_BLOCK_STATE = S5_LANE_GROUPS * S5_STATE


def _s5_kernel(x_ref, win_ref, bre_ref, bim_ref, cre_ref, cim_ref, are_ref, aim_ref, d_ref, wout_ref,
               g_ref, b_ref, wgu_ref, wd_ref, g2_ref, b2_ref, o_ref, hr_s, hi_s, x1_s, *, tb):
    @pl.when(pl.program_id(0) == 0)
    def _():
        hr_s[...] = jnp.zeros_like(hr_s)
        hi_s[...] = jnp.zeros_like(hi_s)
        x1_s[...] = jnp.zeros_like(x1_s)

    ffn = _FfnPieces(x1_s[...], wgu_ref, wd_ref)
    x = x_ref[...]
    u = _dot(x.astype(BF16), win_ref[...])
    ub = u.astype(BF16)
    nt = tb // SUB
    row8 = lax.broadcasted_iota(jnp.int32, (SUB, S5_BLOCK_STATE), 0)

    def tile_scan(zr, zi, mults):
        for i, (mr, mi) in enumerate(mults):
            rr = pltpu.roll(zr, 1 << i, 1)
            ri = pltpu.roll(zi, 1 << i, 1)
            zr, zi = zr + (mr * rr - mi * ri), zi + (mr * ri + mi * rr)
        return zr, zi

    def input_states(j):
        uj = ub[:, j * LANES:(j + 1) * LANES]
        return (_dot(uj, bre_ref[j]).reshape(nt, SUB, S5_BLOCK_STATE),
                _dot(uj, bim_ref[j]).reshape(nt, SUB, S5_BLOCK_STATE))

    y_blocks = []
    for j in range(S5_BLOCKS):
        st = slice(j * S5_BLOCK_STATE, (j + 1) * S5_BLOCK_STATE)
        ar = are_ref[:, st]
        ai = aim_ref[:, st]
        mults = []
        pr, pi = ar, ai
        for i in range(3):
            keep = row8 >= (1 << i)
            mults.append((jnp.where(keep, pr, 0.0), jnp.where(keep, pi, 0.0)))
            pr, pi = pr * pr - pi * pi, 2.0 * pr * pi
        pwr, pwi = tile_scan(jnp.where(row8 == 0, ar, 0.0)[None], jnp.where(row8 == 0, ai, 0.0)[None], mults)
        pwr, pwi = pwr[0], pwi[0]
        sr, si = input_states(j)
        ffn.advance_to((j + 1) / S5_BLOCKS)
        sr, si = tile_scan(sr, si, mults)
        cr = hr_s[:, st]
        ci = hi_s[:, st]
        tiles_r, tiles_i = [], []
        for n in range(nt):
            tr = sr[n] + (pwr * cr - pwi * ci)
            ti = si[n] + (pwr * ci + pwi * cr)
            tiles_r.append(tr)
            tiles_i.append(ti)
            cr, ci = tr[SUB - 1:SUB, :], ti[SUB - 1:SUB, :]
        hr_s[:, st] = cr
        hi_s[:, st] = ci
        hr = jnp.concatenate(tiles_r, axis=0)
        hi = jnp.concatenate(tiles_i, axis=0)
        y_blocks.append(_dot(hr.astype(BF16), cre_ref[j]) - _dot(hi.astype(BF16), cim_ref[j]))
    o_ref[...] = ffn.finish(g2_ref, b2_ref)

    y = jax.nn.gelu(jnp.concatenate(y_blocks, axis=1) + d_ref[...] * u)
    z = _dot(y.astype(BF16), wout_ref[...])
    mix = z[:, :D_MODEL] * _sigmoid(z[:, D_MODEL:])
    x1_s[...] = _layernorm(ALPHA * x + mix, g_ref[...], b_ref[...])


def _s5_block_diag(t):
    a, b = t.shape[1], t.shape[2]
    t = t.reshape(S5_BLOCKS, S5_LANE_GROUPS, a, b)
    eye = jnp.eye(S5_LANE_GROUPS, dtype=t.dtype)
    out = jnp.einsum("jgab,gh->jgahb", t, eye)
    return out.reshape(S5_BLOCKS, S5_LANE_GROUPS * a, S5_LANE_GROUPS * b)


def _s5_layer(x, w_in, a_re, a_im, log_dt, b_re, b_im, c_re, c_im, d_skip, w_out, layer, g, b,
              w_gate_up, w_down, ffn_layer, g2, b2, tb):
    seq = x.shape[0]
    nb = seq // tb
    dt = jnp.exp(log_dt)[:, None]
    mag = jnp.exp(a_re * dt)
    abar_re = mag * jnp.cos(a_im * dt)
    abar_im = mag * jnp.sin(a_im * dt)
    nr = abar_re - 1.0
    ni = abar_im
    den = a_re * a_re + a_im * a_im
    coef_re = (nr * a_re + ni * a_im) / den
    coef_im = (ni * a_re - nr * a_im) / den
    bbar_re = coef_re[..., None] * b_re - coef_im[..., None] * b_im
    bbar_im = coef_re[..., None] * b_im + coef_im[..., None] * b_re
    bre = _s5_block_diag(jnp.swapaxes(bbar_re, 1, 2)).astype(BF16)
    bim = _s5_block_diag(jnp.swapaxes(bbar_im, 1, 2)).astype(BF16)
    cre = _s5_block_diag(jnp.swapaxes(c_re, 1, 2)).astype(BF16)
    cim = _s5_block_diag(jnp.swapaxes(c_im, 1, 2)).astype(BF16)
    n_state = S5_GROUPS * S5_STATE
    spec0, spec1 = _pipeline_specs(tb, D_MODEL, nb, 2)
    vec = _const_spec((1, D_MODEL))
    return pl.pallas_call(
        functools.partial(_s5_kernel, tb=tb),
        out_shape=jax.ShapeDtypeStruct((seq, D_MODEL), F32),
        grid=(nb + 1,),
        in_specs=[spec0, _layer_spec(w_in.shape, layer), _const_spec(bre.shape),
                  _const_spec(bim.shape), _const_spec(cre.shape), _const_spec(cim.shape), _const_spec((1, n_state)),
                  _const_spec((1, n_state)), vec, _layer_spec(w_out.shape, layer), vec, vec,
                  _layer_spec(w_gate_up.shape, ffn_layer), _layer_spec(w_down.shape, ffn_layer), vec, vec],
        out_specs=spec1,
        scratch_shapes=[pltpu.VMEM((1, n_state), F32), pltpu.VMEM((1, n_state), F32),
                        pltpu.VMEM((tb, D_MODEL), F32)],
        compiler_params=_params(),
        name="s5_layer",
    )(x, w_in, bre, bim, cre, cim, abar_re.reshape(1, n_state), abar_im.reshape(1, n_state),
      d_skip.reshape(1, D_MODEL), w_out, g, b, w_gate_up, w_down, g2, b2)


def _hgrn_lower_bounds(lb_logits):
    p = jax.nn.softmax(lb_logits.astype(F32), axis=0)
    c = jnp.cumsum(p, axis=0)
    return c - c[0:1]


def kernel(x, hgrn_w_in, hgrn_norm_w, hgrn_w_out, hgrn_lb_logits, mlstm_w_in, mlstm_conv_w, mlstm_gate_b, mlstm_norm_w, mlstm_w_out, s5_w_in, s5_a_re, s5_a_im, s5_log_dt, s5_b_re, s5_b_im, s5_c_re, s5_c_im, s5_d, s5_w_out, ffn_w_gate_up, ffn_w_down, ln_g, ln_b):
    bsz, seq, _ = x.shape
    tb = math.gcd(seq, ROW_BLOCK)
    assert tb % CHUNK == 0, "sequence length must be a multiple of the recurrence chunk"
    lb_all = _hgrn_lower_bounds(hgrn_lb_logits)
    row = lambda t: t.reshape(1, -1)
    (hgrn_w_in, hgrn_w_out, mlstm_w_in_b, mlstm_w_out, s5_w_in, s5_w_out, ffn_w_gate_up, ffn_w_down) = (
        w.astype(BF16) for w in (hgrn_w_in, hgrn_w_out, mlstm_w_in, mlstm_w_out, s5_w_in, s5_w_out,
                                 ffn_w_gate_up, ffn_w_down))
    outs = []
    for bi in range(bsz):
        h = x[bi]
        for i in range(DEPTH):
            kind = i % N_MIXERS
            j = i // N_MIXERS
            g0, b0 = row(ln_g[i, 0]), row(ln_b[i, 0])
            ffn_args = (ffn_w_gate_up, ffn_w_down, i, row(ln_g[i, 1]), row(ln_b[i, 1]), tb)
            if kind == 0:
                h = _hgrn_layer(h, hgrn_w_in, row(hgrn_norm_w[j]), hgrn_w_out, j, row(lb_all[i]), g0, b0, *ffn_args)
            elif kind == 1:
                w_gate, gate_b = _mlstm_gate_weights(mlstm_w_in[j], mlstm_gate_b[j])
                h = _mlstm_layer(h, mlstm_w_in_b, w_gate, mlstm_conv_w[j], gate_b, row(mlstm_norm_w[j]),
                                 mlstm_w_out, j, g0, b0, *ffn_args)
            else:
                h = _s5_layer(h, s5_w_in, s5_a_re[j], s5_a_im[j], s5_log_dt[j], s5_b_re[j], s5_b_im[j],
                              s5_c_re[j], s5_c_im[j], s5_d[j], s5_w_out, j, g0, b0, *ffn_args)
        outs.append(h)
    return jnp.stack(outs, axis=0)
```

```python
import functools
import math

import jax
import jax.numpy as jnp
from jax import lax
from jax.experimental import pallas as pl
from jax.experimental.pallas import tpu as pltpu

F32 = jnp.float32
BF16 = jnp.bfloat16

D_MODEL = 1024
DEPTH = 4
N_MIXERS = 3
ALPHA = (2.0 * DEPTH) ** 0.25
LN_EPS = 1e-5
HEAD_NORM_EPS = 1e-6

HGRN_HEADS = 8
HGRN_DK = 128
HGRN_DV = 128

MLSTM_HEADS = 8
MLSTM_DV = 128
MLSTM_DQK = 64
MLSTM_CONV = 4
MLSTM_QK_W = MLSTM_HEADS * MLSTM_DQK

S5_GROUP_CH = 16
S5_GROUPS = D_MODEL // S5_GROUP_CH
S5_STATE = 64

FFN_HIDDEN = -(-8 * D_MODEL // (3 * 256)) * 256

HGRN_HEAD_GROUP = 8
CHUNK = 64
SUB = 8
LANES = 128
ROW_BLOCK = 256
VMEM_LIMIT_BYTES = 60 * 1024 * 1024

NEG_INF = float("-inf")


def _dot(a, b):
    return jnp.dot(a, b, preferred_element_type=F32)


def _dot_nt(a, b):
    return lax.dot_general(a, b, (((1,), (1,)), ((), ())), preferred_element_type=F32)


def _dot_tn(a, b):
    return lax.dot_general(a, b, (((0,), (0,)), ((), ())), preferred_element_type=F32)


def _layernorm(y, g, b):
    mu = jnp.mean(y, axis=-1, keepdims=True)
    yc = y - mu
    var = jnp.mean(yc * yc, axis=-1, keepdims=True)
    return yc * lax.rsqrt(var + LN_EPS) * g + b


def _log1p_exp_neg(z):
    return jnp.log(1.0 + jnp.exp(-z))


def _log_sigmoid(z):
    return jnp.minimum(z, 0.0) - _log1p_exp_neg(jnp.abs(z))


LOG2E = 1.4426950408889634


def _sigmoid(z):
    return 1.0 / (1.0 + jnp.exp2(z * -LOG2E))


def _chunk_cumsum(tri, z):
    hi = z.astype(BF16)
    r1 = z - hi.astype(F32)
    mid = r1.astype(BF16)
    lo = (r1 - mid.astype(F32)).astype(BF16)
    return _dot(tri, hi) + _dot(tri, mid) + _dot(tri, lo)


def _const_spec(shape):
    nd = len(shape)
    return pl.BlockSpec(shape, lambda i: (0,) * nd, pipeline_mode=pl.Buffered(1))


def _layer_spec(stacked_shape, layer, cols=None):
    _, rows, width = stacked_shape
    return pl.BlockSpec((None, rows, cols or width), lambda i: (layer, 0, 0), pipeline_mode=pl.Buffered(1))


def _pipeline_specs(tb, width, nb, stages):
    return [pl.BlockSpec((tb, width), functools.partial(lambda i, s: (jnp.clip(i - s, 0, nb - 1), 0), s=s))
            for s in range(stages)]


def _params():
    return pltpu.CompilerParams(dimension_semantics=("arbitrary",), vmem_limit_bytes=VMEM_LIMIT_BYTES)


FFN_PIECE = 256


class _FfnPieces:
    def __init__(self, x, wgu_ref, wd_ref):
        self.x = x
        self.xb = x.astype(BF16)
        self.wgu_ref = wgu_ref
        self.wd_ref = wd_ref
        self.n_up = FFN_HIDDEN // FFN_PIECE
        self.n_down = D_MODEL // FFN_PIECE
        self.done = 0
        self.act = []
        self.out = []

    def advance_to(self, fraction):
        total = self.n_up + self.n_down
        target = min(total, int(fraction * total + 1e-9))
        while self.done < target:
            if self.done < self.n_up:
                lo = self.done * FFN_PIECE
                gate = _dot(self.xb, self.wgu_ref[:, lo:lo + FFN_PIECE])
                up = _dot(self.xb, self.wgu_ref[:, FFN_HIDDEN + lo:FFN_HIDDEN + lo + FFN_PIECE])
                self.act.append((gate * _sigmoid(gate) * up).astype(BF16))
            else:
                if len(self.act) > 1:
                    self.act = [jnp.concatenate(self.act, axis=1)]
                lo = (self.done - self.n_up) * FFN_PIECE
                self.out.append(_dot(self.act[0], self.wd_ref[:, lo:lo + FFN_PIECE]))
            self.done += 1

    def finish(self, g_ref, b_ref):
        self.advance_to(1.0)
        return _layernorm(ALPHA * self.x + jnp.concatenate(self.out, axis=1), g_ref[...], b_ref[...])


def _hgrn_kernel(xf_ref, xb_ref, win_ref, lb_ref, nw_ref, wout_ref, g_ref, b_ref, tri_ref,
                 wgu_ref, wd_ref, g2_ref, b2_ref, o_ref,
                 q_s, k_s, v_s, b_s, gate_s, o_s, state_s, x1_s, *, tb):
    nc = tb // CHUNK

    @pl.when(pl.program_id(0) == 0)
    def _():
        state_s[...] = jnp.zeros_like(state_s)
        for ref in (q_s, k_s, v_s, b_s, gate_s, x1_s):
            ref[...] = jnp.zeros_like(ref)

    ffn = _FfnPieces(x1_s[...], wgu_ref, wd_ref)

    proj = _dot(xf_ref[...].astype(BF16), win_ref[...])
    q = proj[:, 0:D_MODEL]
    f = proj[:, D_MODEL:2 * D_MODEL]
    v = proj[:, 2 * D_MODEL:3 * D_MODEL]
    gt = proj[:, 3 * D_MODEL:4 * D_MODEL]
    lb = lb_ref[...]
    a1 = jnp.log(lb)
    a2 = jnp.log1p(-lb) + _log_sigmoid(f)
    log_f = jnp.maximum(a1, a2) + _log1p_exp_neg(jnp.abs(a1 - a2))
    k = (1.0 - lb) * _sigmoid(-f)
    q = q * _sigmoid(q)
    gate = gt * _sigmoid(gt)
    bcum = _chunk_cumsum(tri_ref[...], log_f) * LOG2E

    row = lax.broadcasted_iota(jnp.int32, (CHUNK, LANES), 0)
    rowi = lax.broadcasted_iota(jnp.int32, (CHUNK, CHUNK), 0)
    coli = lax.broadcasted_iota(jnp.int32, (CHUNK, CHUNK), 1)
    row8 = lax.broadcasted_iota(jnp.int32, (SUB, CHUNK), 0)
    lane8 = lax.broadcasted_iota(jnp.int32, (SUB, CHUNK), 1)
    place = [[((lane8 == blk * SUB + s) & (row8 >= s)).astype(F32) for s in range(SUB)]
             for blk in range(CHUNK // SUB)]
    same_block = {}
    m = SUB
    while m < CHUNK:
        shift = int(math.log2(2 * m))
        same_block[m] = ((rowi >> shift) == (coli >> shift)).astype(F32)
        m *= 2
    nw = nw_ref[...]

    def head_scores(h, rows):
        qc = q_s[h, rows, :]
        kc = k_s[h, rows, :]
        bc = b_s[h, rows, :]
        att = jnp.zeros((CHUNK, CHUNK), F32)
        m = SUB
        while m < CHUNK:
            ref = jnp.concatenate(
                [jnp.broadcast_to(bc[blk * 2 * m + m - 1:blk * 2 * m + m, :], (2 * m, LANES))
                 for blk in range(CHUNK // (2 * m))], axis=0)
            upper = (row & m) != 0
            qh = qc * jnp.exp2(jnp.where(upper, bc - ref, NEG_INF))
            kh = kc * jnp.exp2(jnp.where(upper, NEG_INF, ref - bc))
            pair = _dot_nt(qh.astype(BF16), kh.astype(BF16))
            att = att + pair * same_block[m]
            m *= 2
        tiles = []
        for blk in range(CHUNK // SUB):
            sl = slice(blk * SUB, (blk + 1) * SUB)
            qi, bi = qc[sl], bc[sl]
            tile = jnp.zeros((SUB, CHUNK), F32)
            for s in range(SUB):
                r = rows.start + blk * SUB + s
                ks = jnp.broadcast_to(k_s[h, r:r + 1, :], (SUB, LANES))
                bs = jnp.broadcast_to(b_s[h, r:r + 1, :], (SUB, LANES))
                decay = jnp.exp2(jnp.minimum(bi - bs, 0.0))
                col = jnp.sum(qi * (ks * decay), axis=1, keepdims=True)
                tile = tile + col * place[blk][s]
            tiles.append(tile)
        return (att + jnp.concatenate(tiles, axis=0)).astype(BF16)

    def head_output(h, rows, att):
        st = state_s[h]
        qc = q_s[h, rows, :]
        kc = k_s[h, rows, :]
        vcb = v_s[h, rows, :].astype(BF16)
        bc = b_s[h, rows, :]
        b_last = bc[CHUNK - 1:CHUNK, :]
        o_s[h, rows, :] = _dot(att, vcb) + _dot_nt((qc * jnp.exp2(bc)).astype(BF16), st.astype(BF16))
        k_dec = kc * jnp.exp2(b_last - bc)
        state_s[h] = st * jnp.exp2(b_last) + _dot_tn(vcb, k_dec.astype(BF16))

    n_units = 2 * nc * HGRN_HEADS
    unit = 0
    for c in range(nc):
        rows = slice(c * CHUNK, (c + 1) * CHUNK)
        atts = []
        for h in range(HGRN_HEADS):
            atts.append(head_scores(h, rows))
            unit += 1
            ffn.advance_to(unit / n_units)
        for h in range(HGRN_HEADS):
            head_output(h, rows, atts[h])
            unit += 1
            ffn.advance_to(unit / n_units)
    o_ref[...] = ffn.finish(g2_ref, b2_ref)

    normed = []
    for h in range(HGRN_HEADS):
        o = o_s[h]
        ms = jnp.mean(o * o, axis=-1, keepdims=True)
        normed.append((o * lax.rsqrt(ms + HEAD_NORM_EPS) * nw * gate_s[h]).astype(BF16))
    y = _dot(jnp.concatenate(normed, axis=1), wout_ref[...])
    x1_new = _layernorm(ALPHA * xb_ref[...] + y, g_ref[...], b_ref[...])

    x1_s[...] = x1_new
    for h in range(HGRN_HEADS):
        sl = slice(h * HGRN_DK, (h + 1) * HGRN_DK)
        q_s[h] = q[:, sl]
        k_s[h] = k[:, sl]
        v_s[h] = v[:, sl]
        b_s[h] = bcum[:, sl]
        gate_s[h] = gate[:, sl]


def _chunk_tri(tb):
    i = jnp.arange(tb)
    same = (i[:, None] // CHUNK) == (i[None, :] // CHUNK)
    return (same & (i[None, :] <= i[:, None])).astype(BF16)


def _hgrn_layer(x, w_in, norm_w, w_out, layer, lb, g, b, w_gate_up, w_down, ffn_layer, g2, b2, tb):
    seq = x.shape[0]
    nb = seq // tb
    head_scratch = pltpu.VMEM((HGRN_HEADS, tb, HGRN_DK), F32)
    spec0, spec1, spec2 = _pipeline_specs(tb, D_MODEL, nb, 3)
    vec = _const_spec((1, D_MODEL))
    return pl.pallas_call(
        functools.partial(_hgrn_kernel, tb=tb),
        out_shape=jax.ShapeDtypeStruct((seq, D_MODEL), F32),
        grid=(nb + 2,),
        in_specs=[spec0, spec1, _layer_spec(w_in.shape, layer), vec, _const_spec((1, HGRN_DV)),
                  _layer_spec(w_out.shape, layer), vec, vec, _const_spec((tb, tb)),
                  _layer_spec(w_gate_up.shape, ffn_layer), _layer_spec(w_down.shape, ffn_layer), vec, vec],
        out_specs=spec2,
        scratch_shapes=[head_scratch] * 6 + [pltpu.VMEM((HGRN_HEADS, HGRN_DV, HGRN_DK), F32),
                                             pltpu.VMEM((tb, D_MODEL), F32)],
        compiler_params=_params(),
        name="hgrn_layer",
    )(x, x, w_in, lb, norm_w, w_out, g, b, _chunk_tri(tb), w_gate_up, w_down, g2, b2)


def _mlstm_kernel(xf_ref, xb_ref, win_ref, wg_ref, conv_ref, gb_ref, nw_ref, wout_ref, g_ref, b_ref, tri_ref, sel_ref,
                  wgu_ref, wd_ref, g2_ref, b2_ref, o_ref,
                  pad_s, q_s, k_s, v_s, crow_s, gcol_s, wi_s, em_s, ws_s, wp_s, sigo_s, h_s, c_s, m_s, x1_s, *, tb):
    nc = tb // CHUNK
    halo = SUB

    @pl.when(pl.program_id(0) == 0)
    def _():
        pad_s[0:halo, :] = jnp.zeros((halo, 2 * MLSTM_QK_W), F32)
        c_s[...] = jnp.zeros_like(c_s)
        m_s[...] = jnp.zeros_like(m_s)
        for ref in (q_s, k_s, v_s, crow_s, gcol_s, wi_s, ws_s, wp_s, sigo_s, x1_s):
            ref[...] = jnp.zeros_like(ref)
        em_s[...] = jnp.ones_like(em_s)

    ffn = _FfnPieces(x1_s[...], wgu_ref, wd_ref)
    xb = xf_ref[...].astype(BF16)
    proj = _dot(xb, win_ref[...])
    v_new = proj[:, 2 * MLSTM_QK_W:2 * MLSTM_QK_W + D_MODEL]
    o_pre = proj[:, 2 * MLSTM_QK_W + D_MODEL:]
    pad_s[halo:halo + tb, :] = proj[:, :2 * MLSTM_QK_W]
    conv_w = conv_ref[...]
    qk = pad_s[pl.ds(halo - MLSTM_CONV + 1, tb), :] * conv_w[0:1, :]
    for j in range(1, MLSTM_CONV):
        qk = qk + pad_s[pl.ds(halo - MLSTM_CONV + 1 + j, tb), :] * conv_w[j:j + 1, :]
    pad_s[0:halo, :] = pad_s[tb:tb + halo, :]
    qk = qk * _sigmoid(qk)
    q_new = qk[:, :MLSTM_QK_W]
    k_new = qk[:, MLSTM_QK_W:] * (MLSTM_DQK ** -0.5)
    gates = _dot(xb, wg_ref[...]) + gb_ref[...]
    li = gates[:, :LANES]
    bcum = _chunk_cumsum(tri_ref[...], _log_sigmoid(gates[:, LANES:]))
    cq = li - bcum
    rin = lax.broadcasted_iota(jnp.int32, (tb, LANES), 0) & (CHUNK - 1)
    cm = cq
    k = 1
    while k < CHUNK:
        shifted = pltpu.roll(cm, k, 0)
        cm = jnp.maximum(cm, jnp.where(rin >= k, shifted, NEG_INF))
        k *= 2
    m_prev = m_s[...]
    g_parts, wi_parts, em_parts, ws_parts, wp_parts, crow_new = [], [], [], [], [], []
    for c in range(nc):
        sl = slice(c * CHUNK, (c + 1) * CHUNK)
        mx = jnp.maximum(m_prev, cm[(c + 1) * CHUNK - 1:(c + 1) * CHUNK])
        g_c = jnp.maximum(cm[sl], m_prev)
        g_parts.append(g_c)
        wi_parts.append(jnp.exp(m_prev - g_c))
        em_parts.append(jnp.exp(-(bcum[sl] + g_c)))
        ws_parts.append(jnp.exp(cq[sl] - mx))
        wp_parts.append(jnp.exp(m_prev - mx))
        m_prev = bcum[(c + 1) * CHUNK - 1:(c + 1) * CHUNK] + mx
        crow_new.append(cq[sl].T[0:SUB, :])
    m_s[...] = m_prev
    wp_parts.append(jnp.zeros((SUB - nc, LANES), F32))
    def spread_lanes(tile, terms):
        parts, rest = [], tile
        for _ in range(terms):
            part = rest.astype(BF16)
            parts.append(part)
            rest = rest - part.astype(F32)
        return _dot(jnp.concatenate(parts, axis=1), sel_ref[0:terms * LANES, :])

    g_spread = spread_lanes(jnp.concatenate(g_parts, axis=0), 3)
    w_spread = spread_lanes(jnp.concatenate(wi_parts + em_parts + ws_parts + wp_parts, axis=0), 2)
    sig_o = _sigmoid(o_pre)

    lane_c = lax.broadcasted_iota(jnp.int32, (CHUNK, LANES), 1)
    rowi = lax.broadcasted_iota(jnp.int32, (CHUNK, CHUNK), 0)
    coli = lax.broadcasted_iota(jnp.int32, (CHUNK, CHUNK), 1)
    causal = rowi >= coli
    first_head = lane_c < MLSTM_DQK
    first_rows = lax.broadcasted_iota(jnp.int32, (2 * MLSTM_DQK, 2 * MLSTM_DV), 0) < MLSTM_DQK
    ones_blk = jnp.ones((CHUNK, LANES), BF16)
    nw_all = nw_ref[...]

    n_pairs = MLSTM_HEADS // 2
    n_units = 2 * nc * n_pairs
    unit = 0
    for c in range(nc):
        rows = slice(c * CHUNK, (c + 1) * CHUNK)
        crow = crow_s[c]
        wp_row = wp_s[c:c + 1, :]
        staged = []
        for p in range(n_pairs):
            lanes_p = slice(p * LANES, (p + 1) * LANES)
            cext = c_s[p]
            q_pair = q_s[rows, lanes_p]
            k_pair = k_s[rows, lanes_p]
            hsl = [slice((2 * p + j) * MLSTM_DV, (2 * p + j + 1) * MLSTM_DV) for j in range(2)]
            qm = jnp.concatenate([jnp.where(first_head, q_pair, 0.0), jnp.where(first_head, 0.0, q_pair)],
                                 axis=0).astype(BF16)
            v_ext = [jnp.concatenate([v_s[rows, hsl[j]].astype(BF16), ones_blk], axis=1) for j in range(2)]
            scores = _dot_nt(qm, k_pair.astype(BF16))
            inter = _dot(qm, cext.astype(BF16))
            kw = jnp.concatenate([jnp.where(first_head, k_pair * ws_s[rows, hsl[0]], 0.0),
                                  jnp.where(first_head, 0.0, k_pair * ws_s[rows, hsl[1]])], axis=0).astype(BF16)
            update = _dot_tn(kw, jnp.concatenate(v_ext, axis=0))
            w_prev = jnp.where(first_rows, jnp.tile(wp_row[:, hsl[0]], (1, 2)), jnp.tile(wp_row[:, hsl[1]], (1, 2)))
            c_s[p] = w_prev * cext + update
            staged.append((hsl, v_ext, scores, inter))
            unit += 1
            ffn.advance_to(unit / n_units)
        for p in range(n_pairs):
            hsl, v_ext, scores, inter = staged[p]
            for j in range(2):
                h = 2 * p + j
                hr = slice(j * CHUNK, (j + 1) * CHUNK)
                d_mat = jnp.exp(jnp.where(causal, crow[h:h + 1, :] - gcol_s[rows, h * MLSTM_DV:h * MLSTM_DV + CHUNK],
                                          NEG_INF))
                s = scores[hr] * d_mat
                both = _dot(s.astype(BF16), v_ext[j]) + jnp.tile(wi_s[rows, hsl[j]], (1, 2)) * inter[hr]
                h_s[rows, hsl[j]] = both[:, :MLSTM_DV] / jnp.maximum(jnp.abs(both[:, MLSTM_DV:]), em_s[rows, hsl[j]])
            unit += 1
            ffn.advance_to(unit / n_units)
    o_ref[...] = ffn.finish(g2_ref, b2_ref)

    normed = []
    for h in range(MLSTM_HEADS):
        hsl = slice(h * MLSTM_DV, (h + 1) * MLSTM_DV)
        hh = h_s[:, hsl]
        ms = jnp.mean(hh * hh, axis=-1, keepdims=True)
        normed.append((hh * lax.rsqrt(ms + HEAD_NORM_EPS) * nw_all[:, hsl] * sigo_s[:, hsl]).astype(BF16))
    y = _dot(jnp.concatenate(normed, axis=1), wout_ref[...])
    x1_new = _layernorm(ALPHA * xb_ref[...] + y, g_ref[...], b_ref[...])

    x1_s[...] = x1_new
    q_s[...] = q_new
    k_s[...] = k_new
    v_s[...] = v_new
    sigo_s[...] = sig_o
    for c in range(nc):
        crow_s[c] = crow_new[c]
    gcol_s[...] = g_spread
    wi_s[...] = w_spread[0:tb]
    em_s[...] = w_spread[tb:2 * tb]
    ws_s[...] = w_spread[2 * tb:3 * tb]
    wp_s[...] = w_spread[3 * tb:3 * tb + SUB]


MLSTM_MAIN_W = 2 * MLSTM_QK_W + 2 * D_MODEL


def _mlstm_gate_weights(w_in, gate_b):
    pad = LANES - MLSTM_HEADS
    w_i = jnp.pad(w_in[:, MLSTM_MAIN_W:MLSTM_MAIN_W + MLSTM_HEADS], ((0, 0), (0, pad)))
    w_f = jnp.pad(w_in[:, MLSTM_MAIN_W + MLSTM_HEADS:], ((0, 0), (0, pad)))
    b_i = jnp.pad(gate_b[:MLSTM_HEADS], (0, pad))
    b_f = jnp.pad(gate_b[MLSTM_HEADS:], (0, pad))
    return jnp.concatenate([w_i, w_f], axis=1).astype(BF16), jnp.concatenate([b_i, b_f]).reshape(1, 2 * LANES)


def _lane_spread_matrix():
    src = jnp.arange(LANES)[:, None]
    dst = jnp.arange(D_MODEL)[None, :] // LANES
    return jnp.tile((src == dst).astype(BF16), (3, 1))


def _mlstm_layer(x, w_in, w_gate, conv_w, gate_b, norm_w, w_out, layer, g, b,
                 w_gate_up, w_down, ffn_layer, g2, b2, tb):
    seq = x.shape[0]
    nb = seq // tb
    assert tb // CHUNK <= SUB
    wide = pltpu.VMEM((tb, D_MODEL), F32)
    sel = _lane_spread_matrix()
    spec0, spec1, spec2 = _pipeline_specs(tb, D_MODEL, nb, 3)
    vec = _const_spec((1, D_MODEL))
    return pl.pallas_call(
        functools.partial(_mlstm_kernel, tb=tb),
        out_shape=jax.ShapeDtypeStruct((seq, D_MODEL), F32),
        grid=(nb + 2,),
        in_specs=[spec0, spec1, _layer_spec(w_in.shape, layer, MLSTM_MAIN_W), _const_spec(w_gate.shape),
                  _const_spec(conv_w.shape), _const_spec((1, 2 * LANES)), vec,
                  _layer_spec(w_out.shape, layer), vec, vec, _const_spec((tb, tb)), _const_spec(sel.shape),
                  _layer_spec(w_gate_up.shape, ffn_layer), _layer_spec(w_down.shape, ffn_layer), vec, vec],
        out_specs=spec2,
        scratch_shapes=[pltpu.VMEM((tb + SUB, 2 * MLSTM_QK_W), F32),
                        pltpu.VMEM((tb, MLSTM_QK_W), F32), pltpu.VMEM((tb, MLSTM_QK_W), F32),
                        wide, pltpu.VMEM((tb // CHUNK, SUB, CHUNK), F32), wide, wide, wide, wide,
                        pltpu.VMEM((SUB, D_MODEL), F32), wide, wide,
                        pltpu.VMEM((MLSTM_HEADS // 2, 2 * MLSTM_DQK, 2 * MLSTM_DV), F32),
                        pltpu.VMEM((1, LANES), F32), wide],
        compiler_params=_params(),
        name="mlstm_layer",
    )(x, x, w_in, w_gate, conv_w, gate_b, norm_w, w_out, g, b, _chunk_tri(tb), sel, w_gate_up, w_down, g2, b2)


S5_LANE_GROUPS = LANES // S5_GROUP_CH
S5_BLOCKS = D_MODEL // LANES
S5_BLOCK_STATE = S5_LANE_GROUPS * S5_STATE
S5_TAPS = 4


def _s5_kernel(x_ref, win_ref, bre_ref, bim_ref, cre_ref, cim_ref, are_ref, aim_ref, d_ref, wout_ref,
               g_ref, b_ref, wgu_ref, wd_ref, g2_ref, b2_ref, o_ref, hr_s, hi_s, x1_s, *, tb):
    @pl.when(pl.program_id(0) == 0)
    def _():
        hr_s[...] = jnp.zeros_like(hr_s)
        hi_s[...] = jnp.zeros_like(hi_s)
        x1_s[...] = jnp.zeros_like(x1_s)

    ffn = _FfnPieces(x1_s[...], wgu_ref, wd_ref)
    x = x_ref[...]
    u = _dot(x.astype(BF16), win_ref[...])
    ub = u.astype(BF16)
    nt = tb // SUB
    row8 = lax.broadcasted_iota(jnp.int32, (SUB, S5_BLOCK_STATE), 0)

    def tile_scan(zr, zi, mults, first=0):
        for i in range(first, len(mults)):
            mr, mi = mults[i]
            rr = pltpu.roll(zr, 1 << i, 1)
            ri = pltpu.roll(zi, 1 << i, 1)
            zr, zi = zr + (mr * rr - mi * ri), zi + (mr * ri + mi * rr)
        return zr, zi

    row_u = lax.broadcasted_iota(jnp.int32, (tb, D_MODEL), 0) & (SUB - 1)
    u_taps = [ub] + [jnp.where(row_u >= k, pltpu.roll(u, k, 0), 0.0).astype(BF16) for k in range(1, S5_TAPS)]

    def input_states(j):
        lanes = slice(j * LANES, (j + 1) * LANES)
        uj = jnp.concatenate([t[:, lanes] for t in u_taps], axis=1)
        return (_dot(uj, bre_ref[j]).reshape(nt, SUB, S5_BLOCK_STATE),
                _dot(uj, bim_ref[j]).reshape(nt, SUB, S5_BLOCK_STATE))

    y_blocks = []
    for j in range(S5_BLOCKS):
        st = slice(j * S5_BLOCK_STATE, (j + 1) * S5_BLOCK_STATE)
        ar = are_ref[:, st]
        ai = aim_ref[:, st]
        mults = []
        pr, pi = ar, ai
        for i in range(3):
            keep = row8 >= (1 << i)
            mults.append((jnp.where(keep, pr, 0.0), jnp.where(keep, pi, 0.0)))
            pr, pi = pr * pr - pi * pi, 2.0 * pr * pi
        pwr, pwi = tile_scan(jnp.where(row8 == 0, ar, 0.0)[None], jnp.where(row8 == 0, ai, 0.0)[None], mults)
        pwr, pwi = pwr[0], pwi[0]
        sr, si = input_states(j)
        ffn.advance_to((j + 1) / S5_BLOCKS)
        sr, si = tile_scan(sr, si, mults, first=int(math.log2(S5_TAPS)))
        cr = hr_s[:, st]
        ci = hi_s[:, st]
        tiles_r, tiles_i = [], []
        for n in range(nt):
            tr = sr[n] + (pwr * cr - pwi * ci)
            ti = si[n] + (pwr * ci + pwi * cr)
            tiles_r.append(tr)
            tiles_i.append(ti)
            cr, ci = tr[SUB - 1:SUB, :], ti[SUB - 1:SUB, :]
        hr_s[:, st] = cr
        hi_s[:, st] = ci
        hr = jnp.concatenate(tiles_r, axis=0)
        hi = jnp.concatenate(tiles_i, axis=0)
        y_blocks.append(_dot(hr.astype(BF16), cre_ref[j]) - _dot(hi.astype(BF16), cim_ref[j]))
    o_ref[...] = ffn.finish(g2_ref, b2_ref)

    y = jax.nn.gelu(jnp.concatenate(y_blocks, axis=1) + d_ref[...] * u)
    z = _dot(y.astype(BF16), wout_ref[...])
    mix = z[:, :D_MODEL] * _sigmoid(z[:, D_MODEL:])
    x1_s[...] = _layernorm(ALPHA * x + mix, g_ref[...], b_ref[...])


def _s5_block_diag(t):
    a, b = t.shape[1], t.shape[2]
    t = t.reshape(S5_BLOCKS, S5_LANE_GROUPS, a, b)
    eye = jnp.eye(S5_LANE_GROUPS, dtype=t.dtype)
    out = jnp.einsum("jgab,gh->jgahb", t, eye)
    return out.reshape(S5_BLOCKS, S5_LANE_GROUPS * a, S5_LANE_GROUPS * b)


def _s5_layer(x, w_in, a_re, a_im, log_dt, b_re, b_im, c_re, c_im, d_skip, w_out, layer, g, b,
              w_gate_up, w_down, ffn_layer, g2, b2, tb):
    seq = x.shape[0]
    nb = seq // tb
    dt = jnp.exp(log_dt)[:, None]
    mag = jnp.exp(a_re * dt)
    abar_re = mag * jnp.cos(a_im * dt)
    abar_im = mag * jnp.sin(a_im * dt)
    nr = abar_re - 1.0
    ni = abar_im
    den = a_re * a_re + a_im * a_im
    coef_re = (nr * a_re + ni * a_im) / den
    coef_im = (ni * a_re - nr * a_im) / den
    bbar_re = coef_re[..., None] * b_re - coef_im[..., None] * b_im
    bbar_im = coef_re[..., None] * b_im + coef_im[..., None] * b_re
    taps_re, taps_im = [bbar_re], [bbar_im]
    for _ in range(1, S5_TAPS):
        pr, pi = taps_re[-1], taps_im[-1]
        taps_re.append(abar_re[..., None] * pr - abar_im[..., None] * pi)
        taps_im.append(abar_re[..., None] * pi + abar_im[..., None] * pr)
    stack_taps = lambda taps: jnp.concatenate(
        [_s5_block_diag(jnp.swapaxes(t, 1, 2)) for t in taps], axis=1).astype(BF16)
    bre = stack_taps(taps_re)
    bim = stack_taps(taps_im)
    cre = _s5_block_diag(jnp.swapaxes(c_re, 1, 2)).astype(BF16)
    cim = _s5_block_diag(jnp.swapaxes(c_im, 1, 2)).astype(BF16)
    n_state = S5_GROUPS * S5_STATE
    spec0, spec1 = _pipeline_specs(tb, D_MODEL, nb, 2)
    vec = _const_spec((1, D_MODEL))
    return pl.pallas_call(
        functools.partial(_s5_kernel, tb=tb),
        out_shape=jax.ShapeDtypeStruct((seq, D_MODEL), F32),
        grid=(nb + 1,),
        in_specs=[spec0, _layer_spec(w_in.shape, layer), _const_spec(bre.shape),
                  _const_spec(bim.shape), _const_spec(cre.shape), _const_spec(cim.shape), _const_spec((1, n_state)),
                  _const_spec((1, n_state)), vec, _layer_spec(w_out.shape, layer), vec, vec,
                  _layer_spec(w_gate_up.shape, ffn_layer), _layer_spec(w_down.shape, ffn_layer), vec, vec],
        out_specs=spec1,
        scratch_shapes=[pltpu.VMEM((1, n_state), F32), pltpu.VMEM((1, n_state), F32),
                        pltpu.VMEM((tb, D_MODEL), F32)],
        compiler_params=_params(),
        name="s5_layer",
    )(x, w_in, bre, bim, cre, cim, abar_re.reshape(1, n_state), abar_im.reshape(1, n_state),
      d_skip.reshape(1, D_MODEL), w_out, g, b, w_gate_up, w_down, g2, b2)


def _hgrn_lower_bounds(lb_logits):
    p = jax.nn.softmax(lb_logits.astype(F32), axis=0)
    c = jnp.cumsum(p, axis=0)
    return c - c[0:1]


def kernel(x, hgrn_w_in, hgrn_norm_w, hgrn_w_out, hgrn_lb_logits, mlstm_w_in, mlstm_conv_w, mlstm_gate_b, mlstm_norm_w, mlstm_w_out, s5_w_in, s5_a_re, s5_a_im, s5_log_dt, s5_b_re, s5_b_im, s5_c_re, s5_c_im, s5_d, s5_w_out, ffn_w_gate_up, ffn_w_down, ln_g, ln_b):
    bsz, seq, _ = x.shape
    tb = math.gcd(seq, ROW_BLOCK)
    assert tb % CHUNK == 0, "sequence length must be a multiple of the recurrence chunk"
    lb_all = _hgrn_lower_bounds(hgrn_lb_logits)
    row = lambda t: t.reshape(1, -1)
    (hgrn_w_in, hgrn_w_out, mlstm_w_in_b, mlstm_w_out, s5_w_in, s5_w_out, ffn_w_gate_up, ffn_w_down) = (
        w.astype(BF16) for w in (hgrn_w_in, hgrn_w_out, mlstm_w_in, mlstm_w_out, s5_w_in, s5_w_out,
                                 ffn_w_gate_up, ffn_w_down))
    outs = []
    for bi in range(bsz):
        h = x[bi]
        for i in range(DEPTH):
            kind = i % N_MIXERS
            j = i // N_MIXERS
            g0, b0 = row(ln_g[i, 0]), row(ln_b[i, 0])
            ffn_args = (ffn_w_gate_up, ffn_w_down, i, row(ln_g[i, 1]), row(ln_b[i, 1]), tb)
            if kind == 0:
                h = _hgrn_layer(h, hgrn_w_in, row(hgrn_norm_w[j]), hgrn_w_out, j, row(lb_all[i]), g0, b0, *ffn_args)
            elif kind == 1:
                w_gate, gate_b = _mlstm_gate_weights(mlstm_w_in[j], mlstm_gate_b[j])
                h = _mlstm_layer(h, mlstm_w_in_b, w_gate, mlstm_conv_w[j], gate_b, row(mlstm_norm_w[j]),
                                 mlstm_w_out, j, g0, b0, *ffn_args)
            else:
                h = _s5_layer(h, s5_w_in, s5_a_re[j], s5_a_im[j], s5_log_dt[j], s5_b_re[j], s5_b_im[j],
                              s5_c_re[j], s5_c_im[j], s5_d[j], s5_w_out, j, g0, b0, *ffn_args)
        outs.append(h)
    return jnp.stack(outs, axis=0)
```

```python
import functools
import math

import jax
import jax.numpy as jnp
from jax import lax
from jax.experimental import pallas as pl
from jax.experimental.pallas import tpu as pltpu

F32 = jnp.float32
BF16 = jnp.bfloat16

D_MODEL = 1024
DEPTH = 4
N_MIXERS = 3
ALPHA = (2.0 * DEPTH) ** 0.25
LN_EPS = 1e-5
HEAD_NORM_EPS = 1e-6

HGRN_HEADS = 8
HGRN_DK = 128
HGRN_DV = 128

MLSTM_HEADS = 8
MLSTM_DV = 128
MLSTM_DQK = 64
MLSTM_CONV = 4
MLSTM_QK_W = MLSTM_HEADS * MLSTM_DQK

S5_GROUP_CH = 16
S5_GROUPS = D_MODEL // S5_GROUP_CH
S5_STATE = 64

FFN_HIDDEN = -(-8 * D_MODEL // (3 * 256)) * 256

CHUNK = 64
SUB = 8
LANES = 128
ROW_BLOCK = 256
VMEM_LIMIT_BYTES = 60 * 1024 * 1024

NEG_INF = float("-inf")


def _dot(a, b):
    return jnp.dot(a, b, preferred_element_type=F32)


def _dot_nt(a, b):
    return lax.dot_general(a, b, (((1,), (1,)), ((), ())), preferred_element_type=F32)


def _dot_tn(a, b):
    return lax.dot_general(a, b, (((0,), (0,)), ((), ())), preferred_element_type=F32)


def _layernorm(y, g, b):
    mu = jnp.mean(y, axis=-1, keepdims=True)
    yc = y - mu
    var = jnp.mean(yc * yc, axis=-1, keepdims=True)
    return yc * lax.rsqrt(var + LN_EPS) * g + b


LOG2E = 1.4426950408889634


def _log1p_exp_neg(z):
    return jnp.log(1.0 + jnp.exp2(z * -LOG2E))


def _log_sigmoid(z):
    return jnp.minimum(z, 0.0) - _log1p_exp_neg(jnp.abs(z))


def _sigmoid(z):
    return 1.0 / (1.0 + jnp.exp2(z * -LOG2E))


def _chunk_cumsum(tri, z):
    hi = z.astype(BF16)
    r1 = z - hi.astype(F32)
    mid = r1.astype(BF16)
    lo = (r1 - mid.astype(F32)).astype(BF16)
    return _dot(tri, hi) + _dot(tri, mid) + _dot(tri, lo)


def _const_spec(shape):
    nd = len(shape)
    return pl.BlockSpec(shape, lambda i: (0,) * nd, pipeline_mode=pl.Buffered(1))


def _layer_spec(stacked_shape, layer, cols=None):
    _, rows, width = stacked_shape
    return pl.BlockSpec((None, rows, cols or width), lambda i: (layer, 0, 0), pipeline_mode=pl.Buffered(1))


def _pipeline_specs(tb, width, nb, stages):
    return [pl.BlockSpec((tb, width), functools.partial(lambda i, s: (jnp.clip(i - s, 0, nb - 1), 0), s=s))
            for s in range(stages)]


def _params():
    return pltpu.CompilerParams(dimension_semantics=("arbitrary",), vmem_limit_bytes=VMEM_LIMIT_BYTES)


FFN_PIECE = 256


class _FfnPieces:
    def __init__(self, x, wgu_ref, wd_ref):
        self.x = x
        self.xb = x.astype(BF16)
        self.wgu_ref = wgu_ref
        self.wd_ref = wd_ref
        self.n_up = FFN_HIDDEN // FFN_PIECE
        self.n_down = D_MODEL // FFN_PIECE
        self.done = 0
        self.act = []
        self.out = []

    def advance_to(self, fraction):
        total = self.n_up + self.n_down
        target = min(total, int(fraction * total + 1e-9))
        while self.done < target:
            if self.done < self.n_up:
                lo = self.done * FFN_PIECE
                gate = _dot(self.xb, self.wgu_ref[:, lo:lo + FFN_PIECE])
                up = _dot(self.xb, self.wgu_ref[:, FFN_HIDDEN + lo:FFN_HIDDEN + lo + FFN_PIECE])
                self.act.append((gate * _sigmoid(gate) * up).astype(BF16))
            else:
                if len(self.act) > 1:
                    self.act = [jnp.concatenate(self.act, axis=1)]
                lo = (self.done - self.n_up) * FFN_PIECE
                self.out.append(_dot(self.act[0], self.wd_ref[:, lo:lo + FFN_PIECE]))
            self.done += 1

    def finish(self, g_ref, b_ref):
        self.advance_to(1.0)
        return _layernorm(ALPHA * self.x + jnp.concatenate(self.out, axis=1), g_ref[...], b_ref[...])


def _hgrn_kernel(xf_ref, xb_ref, win_ref, lb_ref, nw_ref, wout_ref, g_ref, b_ref, tri_ref,
                 wgu_ref, wd_ref, g2_ref, b2_ref, o_ref,
                 q_s, k_s, v_s, b_s, gate_s, o_s, state_s, x1_s, *, tb):
    nc = tb // CHUNK

    @pl.when(pl.program_id(0) == 0)
    def _():
        state_s[...] = jnp.zeros_like(state_s)
        for ref in (q_s, k_s, v_s, b_s, gate_s, x1_s):
            ref[...] = jnp.zeros_like(ref)

    ffn = _FfnPieces(x1_s[...], wgu_ref, wd_ref)

    proj = _dot(xf_ref[...].astype(BF16), win_ref[...])
    q = proj[:, 0:D_MODEL]
    f = proj[:, D_MODEL:2 * D_MODEL]
    v = proj[:, 2 * D_MODEL:3 * D_MODEL]
    gt = proj[:, 3 * D_MODEL:4 * D_MODEL]
    lb = lb_ref[...]
    a1 = jnp.log(lb)
    a2 = jnp.log1p(-lb) + _log_sigmoid(f)
    log_f = jnp.maximum(a1, a2) + _log1p_exp_neg(jnp.abs(a1 - a2))
    k = (1.0 - lb) * _sigmoid(-f)
    q = q * _sigmoid(q)
    gate = gt * _sigmoid(gt)
    bcum = _chunk_cumsum(tri_ref[...], log_f) * LOG2E

    row = lax.broadcasted_iota(jnp.int32, (CHUNK, LANES), 0)
    rowi = lax.broadcasted_iota(jnp.int32, (CHUNK, CHUNK), 0)
    coli = lax.broadcasted_iota(jnp.int32, (CHUNK, CHUNK), 1)
    row8 = lax.broadcasted_iota(jnp.int32, (SUB, CHUNK), 0)
    lane8 = lax.broadcasted_iota(jnp.int32, (SUB, CHUNK), 1)
    place = [[((lane8 == blk * SUB + s) & (row8 >= s)).astype(F32) for s in range(SUB)]
             for blk in range(CHUNK // SUB)]
    same_block = {}
    m = SUB
    while m < CHUNK:
        shift = int(math.log2(2 * m))
        same_block[m] = ((rowi >> shift) == (coli >> shift)).astype(F32)
        m *= 2
    nw = nw_ref[...]

    def head_scores(h, rows):
        qc = q_s[h, rows, :]
        kc = k_s[h, rows, :]
        bc = b_s[h, rows, :]
        att = jnp.zeros((CHUNK, CHUNK), F32)
        m = SUB
        while m < CHUNK:
            ref = jnp.concatenate(
                [jnp.broadcast_to(bc[blk * 2 * m + m - 1:blk * 2 * m + m, :], (2 * m, LANES))
                 for blk in range(CHUNK // (2 * m))], axis=0)
            upper = (row & m) != 0
            qh = qc * jnp.exp2(jnp.where(upper, bc - ref, NEG_INF))
            kh = kc * jnp.exp2(jnp.where(upper, NEG_INF, ref - bc))
            pair = _dot_nt(qh.astype(BF16), kh.astype(BF16))
            att = att + pair * same_block[m]
            m *= 2
        tiles = []
        for blk in range(CHUNK // SUB):
            sl = slice(blk * SUB, (blk + 1) * SUB)
            qi, bi = qc[sl], bc[sl]
            tile = jnp.zeros((SUB, CHUNK), F32)
            for s in range(SUB):
                r = rows.start + blk * SUB + s
                ks = jnp.broadcast_to(k_s[h, r:r + 1, :], (SUB, LANES))
                bs = jnp.broadcast_to(b_s[h, r:r + 1, :], (SUB, LANES))
                decay = jnp.exp2(jnp.minimum(bi - bs, 0.0))
                col = jnp.sum(qi * (ks * decay), axis=1, keepdims=True)
                tile = tile + col * place[blk][s]
            tiles.append(tile)
        return (att + jnp.concatenate(tiles, axis=0)).astype(BF16)

    def head_output(h, rows, att):
        st = state_s[h]
        qc = q_s[h, rows, :]
        kc = k_s[h, rows, :]
        vcb = v_s[h, rows, :].astype(BF16)
        bc = b_s[h, rows, :]
        b_last = bc[CHUNK - 1:CHUNK, :]
        o_s[h, rows, :] = _dot(att, vcb) + _dot_nt((qc * jnp.exp2(bc)).astype(BF16), st.astype(BF16))
        k_dec = kc * jnp.exp2(b_last - bc)
        state_s[h] = st * jnp.exp2(b_last) + _dot_tn(vcb, k_dec.astype(BF16))

    n_units = 2 * nc * HGRN_HEADS
    unit = 0
    for c in range(nc):
        rows = slice(c * CHUNK, (c + 1) * CHUNK)
        atts = []
        for h in range(HGRN_HEADS):
            atts.append(head_scores(h, rows))
            unit += 1
            ffn.advance_to(unit / n_units)
        for h in range(HGRN_HEADS):
            head_output(h, rows, atts[h])
            unit += 1
            ffn.advance_to(unit / n_units)
    o_ref[...] = ffn.finish(g2_ref, b2_ref)

    normed = []
    for h in range(HGRN_HEADS):
        o = o_s[h]
        ms = jnp.mean(o * o, axis=-1, keepdims=True)
        normed.append((o * lax.rsqrt(ms + HEAD_NORM_EPS) * nw * gate_s[h]).astype(BF16))
    y = _dot(jnp.concatenate(normed, axis=1), wout_ref[...])
    x1_new = _layernorm(ALPHA * xb_ref[...] + y, g_ref[...], b_ref[...])

    x1_s[...] = x1_new
    for h in range(HGRN_HEADS):
        sl = slice(h * HGRN_DK, (h + 1) * HGRN_DK)
        q_s[h] = q[:, sl]
        k_s[h] = k[:, sl]
        v_s[h] = v[:, sl]
        b_s[h] = bcum[:, sl]
        gate_s[h] = gate[:, sl]


def _chunk_tri(tb):
    i = jnp.arange(tb)
    same = (i[:, None] // CHUNK) == (i[None, :] // CHUNK)
    return (same & (i[None, :] <= i[:, None])).astype(BF16)


def _hgrn_layer(x, w_in, norm_w, w_out, layer, lb, g, b, w_gate_up, w_down, ffn_layer, g2, b2, tb):
    seq = x.shape[0]
    nb = seq // tb
    head_scratch = pltpu.VMEM((HGRN_HEADS, tb, HGRN_DK), F32)
    spec0, spec1, spec2 = _pipeline_specs(tb, D_MODEL, nb, 3)
    vec = _const_spec((1, D_MODEL))
    return pl.pallas_call(
        functools.partial(_hgrn_kernel, tb=tb),
        out_shape=jax.ShapeDtypeStruct((seq, D_MODEL), F32),
        grid=(nb + 2,),
        in_specs=[spec0, spec1, _layer_spec(w_in.shape, layer), vec, _const_spec((1, HGRN_DV)),
                  _layer_spec(w_out.shape, layer), vec, vec, _const_spec((tb, tb)),
                  _layer_spec(w_gate_up.shape, ffn_layer), _layer_spec(w_down.shape, ffn_layer), vec, vec],
        out_specs=spec2,
        scratch_shapes=[head_scratch] * 6 + [pltpu.VMEM((HGRN_HEADS, HGRN_DV, HGRN_DK), F32),
                                             pltpu.VMEM((tb, D_MODEL), F32)],
        compiler_params=_params(),
        name="hgrn_layer",
    )(x, x, w_in, lb, norm_w, w_out, g, b, _chunk_tri(tb), w_gate_up, w_down, g2, b2)


def _mlstm_kernel(xf_ref, xb_ref, win_ref, wg_ref, conv_ref, gb_ref, nw_ref, wout_ref, g_ref, b_ref, tri_ref, sel_ref,
                  wgu_ref, wd_ref, g2_ref, b2_ref, o_ref,
                  pad_s, q_s, k_s, v_s, crow_s, gcol_s, wi_s, em_s, ws_s, wp_s, sigo_s, h_s, c_s, m_s, x1_s, *, tb):
    nc = tb // CHUNK
    halo = SUB

    @pl.when(pl.program_id(0) == 0)
    def _():
        pad_s[0:halo, :] = jnp.zeros((halo, 2 * MLSTM_QK_W), F32)
        c_s[...] = jnp.zeros_like(c_s)
        m_s[...] = jnp.zeros_like(m_s)
        for ref in (q_s, k_s, v_s, crow_s, gcol_s, wi_s, ws_s, wp_s, sigo_s, x1_s):
            ref[...] = jnp.zeros_like(ref)
        em_s[...] = jnp.ones_like(em_s)

    ffn = _FfnPieces(x1_s[...], wgu_ref, wd_ref)
    xb = xf_ref[...].astype(BF16)
    proj = _dot(xb, win_ref[...])
    v_new = proj[:, 2 * MLSTM_QK_W:2 * MLSTM_QK_W + D_MODEL]
    o_pre = proj[:, 2 * MLSTM_QK_W + D_MODEL:]
    pad_s[halo:halo + tb, :] = proj[:, :2 * MLSTM_QK_W]
    conv_w = conv_ref[...]
    qk = pad_s[pl.ds(halo - MLSTM_CONV + 1, tb), :] * conv_w[0:1, :]
    for j in range(1, MLSTM_CONV):
        qk = qk + pad_s[pl.ds(halo - MLSTM_CONV + 1 + j, tb), :] * conv_w[j:j + 1, :]
    pad_s[0:halo, :] = pad_s[tb:tb + halo, :]
    qk = qk * _sigmoid(qk)
    q_new = qk[:, :MLSTM_QK_W]
    k_new = qk[:, MLSTM_QK_W:] * (MLSTM_DQK ** -0.5)
    gates = _dot(xb, wg_ref[...]) + gb_ref[...]
    li = gates[:, :LANES]
    bcum = _chunk_cumsum(tri_ref[...], _log_sigmoid(gates[:, LANES:]))
    cq = li - bcum
    rin = lax.broadcasted_iota(jnp.int32, (tb, LANES), 0) & (CHUNK - 1)
    cm = cq
    k = 1
    while k < CHUNK:
        shifted = pltpu.roll(cm, k, 0)
        cm = jnp.maximum(cm, jnp.where(rin >= k, shifted, NEG_INF))
        k *= 2
    m_prev = m_s[...]
    g_parts, wi_parts, em_parts, ws_parts, wp_parts, crow_new = [], [], [], [], [], []
    for c in range(nc):
        sl = slice(c * CHUNK, (c + 1) * CHUNK)
        mx = jnp.maximum(m_prev, cm[(c + 1) * CHUNK - 1:(c + 1) * CHUNK])
        g_c = jnp.maximum(cm[sl], m_prev)
        g_parts.append(g_c)
        wi_parts.append(jnp.exp(m_prev - g_c))
        em_parts.append(jnp.exp(-(bcum[sl] + g_c)))
        ws_parts.append(jnp.exp(cq[sl] - mx))
        wp_parts.append(jnp.exp(m_prev - mx))
        m_prev = bcum[(c + 1) * CHUNK - 1:(c + 1) * CHUNK] + mx
        crow_new.append(cq[sl].T[0:SUB, :])
    m_s[...] = m_prev
    wp_parts.append(jnp.zeros((SUB - nc, LANES), F32))
    def spread_lanes(tile, terms):
        parts, rest = [], tile
        for _ in range(terms):
            part = rest.astype(BF16)
            parts.append(part)
            rest = rest - part.astype(F32)
        return _dot(jnp.concatenate(parts, axis=1), sel_ref[0:terms * LANES, :])

    g_spread = spread_lanes(jnp.concatenate(g_parts, axis=0), 3)
    w_spread = spread_lanes(jnp.concatenate(wi_parts + em_parts + ws_parts + wp_parts, axis=0), 2)
    sig_o = _sigmoid(o_pre)

    lane_c = lax.broadcasted_iota(jnp.int32, (CHUNK, LANES), 1)
    rowi = lax.broadcasted_iota(jnp.int32, (CHUNK, CHUNK), 0)
    coli = lax.broadcasted_iota(jnp.int32, (CHUNK, CHUNK), 1)
    causal = rowi >= coli
    first_head = lane_c < MLSTM_DQK
    first_rows = lax.broadcasted_iota(jnp.int32, (2 * MLSTM_DQK, 2 * MLSTM_DV), 0) < MLSTM_DQK
    ones_blk = jnp.ones((CHUNK, LANES), BF16)
    nw_all = nw_ref[...]

    n_pairs = MLSTM_HEADS // 2
    n_units = 2 * nc * n_pairs
    unit = 0
    for c in range(nc):
        rows = slice(c * CHUNK, (c + 1) * CHUNK)
        crow = crow_s[c]
        wp_row = wp_s[c:c + 1, :]
        staged = []
        for p in range(n_pairs):
            lanes_p = slice(p * LANES, (p + 1) * LANES)
            cext = c_s[p]
            q_pair = q_s[rows, lanes_p]
            k_pair = k_s[rows, lanes_p]
            hsl = [slice((2 * p + j) * MLSTM_DV, (2 * p + j + 1) * MLSTM_DV) for j in range(2)]
            qm = jnp.concatenate([jnp.where(first_head, q_pair, 0.0), jnp.where(first_head, 0.0, q_pair)],
                                 axis=0).astype(BF16)
            v_ext = [jnp.concatenate([v_s[rows, hsl[j]].astype(BF16), ones_blk], axis=1) for j in range(2)]
            scores = _dot_nt(qm, k_pair.astype(BF16))
            inter = _dot(qm, cext.astype(BF16))
            kw = jnp.concatenate([jnp.where(first_head, k_pair * ws_s[rows, hsl[0]], 0.0),
                                  jnp.where(first_head, 0.0, k_pair * ws_s[rows, hsl[1]])], axis=0).astype(BF16)
            update = _dot_tn(kw, jnp.concatenate(v_ext, axis=0))
            w_prev = jnp.where(first_rows, jnp.tile(wp_row[:, hsl[0]], (1, 2)), jnp.tile(wp_row[:, hsl[1]], (1, 2)))
            c_s[p] = w_prev * cext + update
            staged.append((hsl, v_ext, scores, inter))
            unit += 1
            ffn.advance_to(unit / n_units)
        for p in range(n_pairs):
            hsl, v_ext, scores, inter = staged[p]
            for j in range(2):
                h = 2 * p + j
                hr = slice(j * CHUNK, (j + 1) * CHUNK)
                d_mat = jnp.exp(jnp.where(causal, crow[h:h + 1, :] - gcol_s[rows, h * MLSTM_DV:h * MLSTM_DV + CHUNK],
                                          NEG_INF))
                s = scores[hr] * d_mat
                both = _dot(s.astype(BF16), v_ext[j]) + jnp.tile(wi_s[rows, hsl[j]], (1, 2)) * inter[hr]
                h_s[rows, hsl[j]] = both[:, :MLSTM_DV] / jnp.maximum(jnp.abs(both[:, MLSTM_DV:]), em_s[rows, hsl[j]])
            unit += 1
            ffn.advance_to(unit / n_units)
    o_ref[...] = ffn.finish(g2_ref, b2_ref)

    normed = []
    for h in range(MLSTM_HEADS):
        hsl = slice(h * MLSTM_DV, (h + 1) * MLSTM_DV)
        hh = h_s[:, hsl]
        ms = jnp.mean(hh * hh, axis=-1, keepdims=True)
        normed.append((hh * lax.rsqrt(ms + HEAD_NORM_EPS) * nw_all[:, hsl] * sigo_s[:, hsl]).astype(BF16))
    y = _dot(jnp.concatenate(normed, axis=1), wout_ref[...])
    x1_new = _layernorm(ALPHA * xb_ref[...] + y, g_ref[...], b_ref[...])

    x1_s[...] = x1_new
    q_s[...] = q_new
    k_s[...] = k_new
    v_s[...] = v_new
    sigo_s[...] = sig_o
    for c in range(nc):
        crow_s[c] = crow_new[c]
    gcol_s[...] = g_spread
    wi_s[...] = w_spread[0:tb]
    em_s[...] = w_spread[tb:2 * tb]
    ws_s[...] = w_spread[2 * tb:3 * tb]
    wp_s[...] = w_spread[3 * tb:3 * tb + SUB]


MLSTM_MAIN_W = 2 * MLSTM_QK_W + 2 * D_MODEL


def _mlstm_gate_weights(w_in, gate_b):
    pad = LANES - MLSTM_HEADS
    w_i = jnp.pad(w_in[:, MLSTM_MAIN_W:MLSTM_MAIN_W + MLSTM_HEADS], ((0, 0), (0, pad)))
    w_f = jnp.pad(w_in[:, MLSTM_MAIN_W + MLSTM_HEADS:], ((0, 0), (0, pad)))
    b_i = jnp.pad(gate_b[:MLSTM_HEADS], (0, pad))
    b_f = jnp.pad(gate_b[MLSTM_HEADS:], (0, pad))
    return jnp.concatenate([w_i, w_f], axis=1).astype(BF16), jnp.concatenate([b_i, b_f]).reshape(1, 2 * LANES)


def _lane_spread_matrix():
    src = jnp.arange(LANES)[:, None]
    dst = jnp.arange(D_MODEL)[None, :] // LANES
    return jnp.tile((src == dst).astype(BF16), (3, 1))


def _mlstm_layer(x, w_in, w_gate, conv_w, gate_b, norm_w, w_out, layer, g, b,
                 w_gate_up, w_down, ffn_layer, g2, b2, tb):
    seq = x.shape[0]
    nb = seq // tb
    assert tb // CHUNK <= SUB
    wide = pltpu.VMEM((tb, D_MODEL), F32)
    sel = _lane_spread_matrix()
    spec0, spec1, spec2 = _pipeline_specs(tb, D_MODEL, nb, 3)
    vec = _const_spec((1, D_MODEL))
    return pl.pallas_call(
        functools.partial(_mlstm_kernel, tb=tb),
        out_shape=jax.ShapeDtypeStruct((seq, D_MODEL), F32),
        grid=(nb + 2,),
        in_specs=[spec0, spec1, _layer_spec(w_in.shape, layer, MLSTM_MAIN_W), _const_spec(w_gate.shape),
                  _const_spec(conv_w.shape), _const_spec((1, 2 * LANES)), vec,
                  _layer_spec(w_out.shape, layer), vec, vec, _const_spec((tb, tb)), _const_spec(sel.shape),
                  _layer_spec(w_gate_up.shape, ffn_layer), _layer_spec(w_down.shape, ffn_layer), vec, vec],
        out_specs=spec2,
        scratch_shapes=[pltpu.VMEM((tb + SUB, 2 * MLSTM_QK_W), F32),
                        pltpu.VMEM((tb, MLSTM_QK_W), F32), pltpu.VMEM((tb, MLSTM_QK_W), F32),
                        wide, pltpu.VMEM((tb // CHUNK, SUB, CHUNK), F32), wide, wide, wide, wide,
                        pltpu.VMEM((SUB, D_MODEL), F32), wide, wide,
                        pltpu.VMEM((MLSTM_HEADS // 2, 2 * MLSTM_DQK, 2 * MLSTM_DV), F32),
                        pltpu.VMEM((1, LANES), F32), wide],
        compiler_params=_params(),
        name="mlstm_layer",
    )(x, x, w_in, w_gate, conv_w, gate_b, norm_w, w_out, g, b, _chunk_tri(tb), sel, w_gate_up, w_down, g2, b2)


S5_LANE_GROUPS = LANES // S5_GROUP_CH
S5_BLOCKS = D_MODEL // LANES
S5_BLOCK_STATE = S5_LANE_GROUPS * S5_STATE
S5_TAPS = 4


def _s5_kernel(x_ref, win_ref, bre_ref, bim_ref, cre_ref, cim_ref, are_ref, aim_ref, d_ref, wout_ref,
               g_ref, b_ref, wgu_ref, wd_ref, g2_ref, b2_ref, o_ref, hr_s, hi_s, x1_s, *, tb):
    @pl.when(pl.program_id(0) == 0)
    def _():
        hr_s[...] = jnp.zeros_like(hr_s)
        hi_s[...] = jnp.zeros_like(hi_s)
        x1_s[...] = jnp.zeros_like(x1_s)

    ffn = _FfnPieces(x1_s[...], wgu_ref, wd_ref)
    x = x_ref[...]
    u = _dot(x.astype(BF16), win_ref[...])
    ub = u.astype(BF16)
    nt = tb // SUB
    row8 = lax.broadcasted_iota(jnp.int32, (SUB, S5_BLOCK_STATE), 0)

    def tile_scan(zr, zi, mults, first=0):
        for i in range(first, len(mults)):
            mr, mi = mults[i]
            rr = pltpu.roll(zr, 1 << i, 1)
            ri = pltpu.roll(zi, 1 << i, 1)
            zr, zi = zr + (mr * rr - mi * ri), zi + (mr * ri + mi * rr)
        return zr, zi

    row_u = lax.broadcasted_iota(jnp.int32, (tb, D_MODEL), 0) & (SUB - 1)
    u_taps = [ub] + [jnp.where(row_u >= k, pltpu.roll(u, k, 0), 0.0).astype(BF16) for k in range(1, S5_TAPS)]

    def input_states(j):
        lanes = slice(j * LANES, (j + 1) * LANES)
        uj = jnp.concatenate([t[:, lanes] for t in u_taps], axis=1)
        return (_dot(uj, bre_ref[j]).reshape(nt, SUB, S5_BLOCK_STATE),
                _dot(uj, bim_ref[j]).reshape(nt, SUB, S5_BLOCK_STATE))

    y_blocks = []
    for j in range(S5_BLOCKS):
        st = slice(j * S5_BLOCK_STATE, (j + 1) * S5_BLOCK_STATE)
        ar = are_ref[:, st]
        ai = aim_ref[:, st]
        mults = []
        pr, pi = ar, ai
        for i in range(3):
            keep = row8 >= (1 << i)
            mults.append((jnp.where(keep, pr, 0.0), jnp.where(keep, pi, 0.0)))
            pr, pi = pr * pr - pi * pi, 2.0 * pr * pi
        pwr, pwi = tile_scan(jnp.where(row8 == 0, ar, 0.0)[None], jnp.where(row8 == 0, ai, 0.0)[None], mults)
        pwr, pwi = pwr[0], pwi[0]
        sr, si = input_states(j)
        ffn.advance_to((j + 1) / S5_BLOCKS)
        sr, si = tile_scan(sr, si, mults, first=int(math.log2(S5_TAPS)))
        cr = hr_s[:, st]
        ci = hi_s[:, st]
        tiles_r, tiles_i = [], []
        for n in range(nt):
            tr = sr[n] + (pwr * cr - pwi * ci)
            ti = si[n] + (pwr * ci + pwi * cr)
            tiles_r.append(tr)
            tiles_i.append(ti)
            cr, ci = tr[SUB - 1:SUB, :], ti[SUB - 1:SUB, :]
        hr_s[:, st] = cr
        hi_s[:, st] = ci
        hr = jnp.concatenate(tiles_r, axis=0)
        hi = jnp.concatenate(tiles_i, axis=0)
        y_blocks.append(_dot(hr.astype(BF16), cre_ref[j]) - _dot(hi.astype(BF16), cim_ref[j]))
    o_ref[...] = ffn.finish(g2_ref, b2_ref)

    y = jax.nn.gelu(jnp.concatenate(y_blocks, axis=1) + d_ref[...] * u)
    z = _dot(y.astype(BF16), wout_ref[...])
    mix = z[:, :D_MODEL] * _sigmoid(z[:, D_MODEL:])
    x1_s[...] = _layernorm(ALPHA * x + mix, g_ref[...], b_ref[...])


def _s5_block_diag(t):
    a, b = t.shape[1], t.shape[2]
    t = t.reshape(S5_BLOCKS, S5_LANE_GROUPS, a, b)
    eye = jnp.eye(S5_LANE_GROUPS, dtype=t.dtype)
    out = jnp.einsum("jgab,gh->jgahb", t, eye)
    return out.reshape(S5_BLOCKS, S5_LANE_GROUPS * a, S5_LANE_GROUPS * b)


def _s5_layer(x, w_in, a_re, a_im, log_dt, b_re, b_im, c_re, c_im, d_skip, w_out, layer, g, b,
              w_gate_up, w_down, ffn_layer, g2, b2, tb):
    seq = x.shape[0]
    nb = seq // tb
    dt = jnp.exp(log_dt)[:, None]
    mag = jnp.exp(a_re * dt)
    abar_re = mag * jnp.cos(a_im * dt)
    abar_im = mag * jnp.sin(a_im * dt)
    nr = abar_re - 1.0
    ni = abar_im
    den = a_re * a_re + a_im * a_im
    coef_re = (nr * a_re + ni * a_im) / den
    coef_im = (ni * a_re - nr * a_im) / den
    bbar_re = coef_re[..., None] * b_re - coef_im[..., None] * b_im
    bbar_im = coef_re[..., None] * b_im + coef_im[..., None] * b_re
    taps_re, taps_im = [bbar_re], [bbar_im]
    for _ in range(1, S5_TAPS):
        pr, pi = taps_re[-1], taps_im[-1]
        taps_re.append(abar_re[..., None] * pr - abar_im[..., None] * pi)
        taps_im.append(abar_re[..., None] * pi + abar_im[..., None] * pr)
    stack_taps = lambda taps: jnp.concatenate(
        [_s5_block_diag(jnp.swapaxes(t, 1, 2)) for t in taps], axis=1).astype(BF16)
    bre = stack_taps(taps_re)
    bim = stack_taps(taps_im)
    cre = _s5_block_diag(jnp.swapaxes(c_re, 1, 2)).astype(BF16)
    cim = _s5_block_diag(jnp.swapaxes(c_im, 1, 2)).astype(BF16)
    n_state = S5_GROUPS * S5_STATE
    spec0, spec1 = _pipeline_specs(tb, D_MODEL, nb, 2)
    vec = _const_spec((1, D_MODEL))
    return pl.pallas_call(
        functools.partial(_s5_kernel, tb=tb),
        out_shape=jax.ShapeDtypeStruct((seq, D_MODEL), F32),
        grid=(nb + 1,),
        in_specs=[spec0, _layer_spec(w_in.shape, layer), _const_spec(bre.shape),
                  _const_spec(bim.shape), _const_spec(cre.shape), _const_spec(cim.shape), _const_spec((1, n_state)),
                  _const_spec((1, n_state)), vec, _layer_spec(w_out.shape, layer), vec, vec,
                  _layer_spec(w_gate_up.shape, ffn_layer), _layer_spec(w_down.shape, ffn_layer), vec, vec],
        out_specs=spec1,
        scratch_shapes=[pltpu.VMEM((1, n_state), F32), pltpu.VMEM((1, n_state), F32),
                        pltpu.VMEM((tb, D_MODEL), F32)],
        compiler_params=_params(),
        name="s5_layer",
    )(x, w_in, bre, bim, cre, cim, abar_re.reshape(1, n_state), abar_im.reshape(1, n_state),
      d_skip.reshape(1, D_MODEL), w_out, g, b, w_gate_up, w_down, g2, b2)


def _hgrn_lower_bounds(lb_logits):
    p = jax.nn.softmax(lb_logits.astype(F32), axis=0)
    c = jnp.cumsum(p, axis=0)
    return c - c[0:1]


def kernel(x, hgrn_w_in, hgrn_norm_w, hgrn_w_out, hgrn_lb_logits, mlstm_w_in, mlstm_conv_w, mlstm_gate_b, mlstm_norm_w, mlstm_w_out, s5_w_in, s5_a_re, s5_a_im, s5_log_dt, s5_b_re, s5_b_im, s5_c_re, s5_c_im, s5_d, s5_w_out, ffn_w_gate_up, ffn_w_down, ln_g, ln_b):
    bsz, seq, _ = x.shape
    tb = math.gcd(seq, ROW_BLOCK)
    assert tb % CHUNK == 0, "sequence length must be a multiple of the recurrence chunk"
    lb_all = _hgrn_lower_bounds(hgrn_lb_logits)
    row = lambda t: t.reshape(1, -1)
    (hgrn_w_in, hgrn_w_out, mlstm_w_in_b, mlstm_w_out, s5_w_in, s5_w_out, ffn_w_gate_up, ffn_w_down) = (
        w.astype(BF16) for w in (hgrn_w_in, hgrn_w_out, mlstm_w_in, mlstm_w_out, s5_w_in, s5_w_out,
                                 ffn_w_gate_up, ffn_w_down))
    outs = []
    for bi in range(bsz):
        h = x[bi]
        for i in range(DEPTH):
            kind = i % N_MIXERS
            j = i // N_MIXERS
            g0, b0 = row(ln_g[i, 0]), row(ln_b[i, 0])
            ffn_args = (ffn_w_gate_up, ffn_w_down, i, row(ln_g[i, 1]), row(ln_b[i, 1]), tb)
            if kind == 0:
                h = _hgrn_layer(h, hgrn_w_in, row(hgrn_norm_w[j]), hgrn_w_out, j, row(lb_all[i]), g0, b0, *ffn_args)
            elif kind == 1:
                w_gate, gate_b = _mlstm_gate_weights(mlstm_w_in[j], mlstm_gate_b[j])
                h = _mlstm_layer(h, mlstm_w_in_b, w_gate, mlstm_conv_w[j], gate_b, row(mlstm_norm_w[j]),
                                 mlstm_w_out, j, g0, b0, *ffn_args)
            else:
                h = _s5_layer(h, s5_w_in, s5_a_re[j], s5_a_im[j], s5_log_dt[j], s5_b_re[j], s5_b_im[j],
                              s5_c_re[j], s5_c_im[j], s5_d[j], s5_w_out, j, g0, b0, *ffn_args)
        outs.append(h)
    return jnp.stack(outs, axis=0)
```

```python
import functools
import math

import jax
import jax.numpy as jnp
from jax import lax
from jax.experimental import pallas as pl
from jax.experimental.pallas import tpu as pltpu

F32 = jnp.float32
BF16 = jnp.bfloat16

D_MODEL = 1024
DEPTH = 4
N_MIXERS = 3
ALPHA = (2.0 * DEPTH) ** 0.25
LN_EPS = 1e-5
HEAD_NORM_EPS = 1e-6

HGRN_HEADS = 8
HGRN_DK = 128
HGRN_DV = 128

MLSTM_HEADS = 8
MLSTM_DV = 128
MLSTM_DQK = 64
MLSTM_CONV = 4
MLSTM_QK_W = MLSTM_HEADS * MLSTM_DQK

S5_GROUP_CH = 16
S5_GROUPS = D_MODEL // S5_GROUP_CH
S5_STATE = 64

FFN_HIDDEN = -(-8 * D_MODEL // (3 * 256)) * 256

CHUNK = 64
SUB = 8
LANES = 128
ROW_BLOCK = 256
VMEM_LIMIT_BYTES = 60 * 1024 * 1024

NEG_INF = float("-inf")
BF16_MAX = float(jnp.finfo(jnp.bfloat16).max)


def _dot(a, b):
    return jnp.dot(a, b, preferred_element_type=F32)


def _dot_nt(a, b):
    return lax.dot_general(a, b, (((1,), (1,)), ((), ())), preferred_element_type=F32)


def _dot_tn(a, b):
    return lax.dot_general(a, b, (((0,), (0,)), ((), ())), preferred_element_type=F32)


def _layernorm(y, g, b):
    mu = jnp.mean(y, axis=-1, keepdims=True)
    yc = y - mu
    var = jnp.mean(yc * yc, axis=-1, keepdims=True)
    return yc * lax.rsqrt(var + LN_EPS) * g + b


LOG2E = 1.4426950408889634


def _log1p_exp_neg(z):
    return jnp.log(1.0 + jnp.exp2(z * -LOG2E))


def _log_sigmoid(z):
    return jnp.minimum(z, 0.0) - _log1p_exp_neg(jnp.abs(z))


def _sigmoid(z):
    return 1.0 / (1.0 + jnp.exp2(z * -LOG2E))


def _chunk_cumsum(tri, z):
    hi = z.astype(BF16)
    r1 = z - hi.astype(F32)
    mid = r1.astype(BF16)
    lo = (r1 - mid.astype(F32)).astype(BF16)
    return _dot(tri, hi) + _dot(tri, mid) + _dot(tri, lo)


def _const_spec(shape):
    nd = len(shape)
    return pl.BlockSpec(shape, lambda i: (0,) * nd, pipeline_mode=pl.Buffered(1))


def _layer_spec(stacked_shape, layer, cols=None):
    _, rows, width = stacked_shape
    return pl.BlockSpec((None, rows, cols or width), lambda i: (layer, 0, 0), pipeline_mode=pl.Buffered(1))


def _pipeline_specs(tb, width, nb, stages):
    return [pl.BlockSpec((tb, width), functools.partial(lambda i, s: (jnp.clip(i - s, 0, nb - 1), 0), s=s))
            for s in range(stages)]


def _params():
    return pltpu.CompilerParams(dimension_semantics=("arbitrary",), vmem_limit_bytes=VMEM_LIMIT_BYTES)


FFN_PIECE = 256


class _FfnPieces:
    def __init__(self, x, wgu_ref, wd_ref):
        self.x = x
        self.xb = x.astype(BF16)
        self.wgu_ref = wgu_ref
        self.wd_ref = wd_ref
        self.n_up = FFN_HIDDEN // FFN_PIECE
        self.n_down = D_MODEL // FFN_PIECE
        self.done = 0
        self.act = []
        self.out = []

    def advance_to(self, fraction):
        total = self.n_up + self.n_down
        target = min(total, int(fraction * total + 1e-9))
        while self.done < target:
            if self.done < self.n_up:
                lo = self.done * FFN_PIECE
                gate = _dot(self.xb, self.wgu_ref[:, lo:lo + FFN_PIECE])
                up = _dot(self.xb, self.wgu_ref[:, FFN_HIDDEN + lo:FFN_HIDDEN + lo + FFN_PIECE])
                self.act.append((gate * _sigmoid(gate) * up).astype(BF16))
            else:
                if len(self.act) > 1:
                    self.act = [jnp.concatenate(self.act, axis=1)]
                lo = (self.done - self.n_up) * FFN_PIECE
                self.out.append(_dot(self.act[0], self.wd_ref[:, lo:lo + FFN_PIECE]))
            self.done += 1

    def finish(self, g_ref, b_ref):
        self.advance_to(1.0)
        return _layernorm(ALPHA * self.x + jnp.concatenate(self.out, axis=1), g_ref[...], b_ref[...])


def _hgrn_kernel(xf_ref, xb_ref, win_ref, lb_ref, nw_ref, wout_ref, g_ref, b_ref, tri_ref,
                 wgu_ref, wd_ref, g2_ref, b2_ref, o_ref,
                 q_s, k_s, v_s, b_s, gate_s, o_s, state_s, x1_s, *, tb):
    nc = tb // CHUNK

    @pl.when(pl.program_id(0) == 0)
    def _():
        state_s[...] = jnp.zeros_like(state_s)
        for ref in (q_s, k_s, v_s, b_s, gate_s, x1_s):
            ref[...] = jnp.zeros_like(ref)

    ffn = _FfnPieces(x1_s[...], wgu_ref, wd_ref)

    proj = _dot(xf_ref[...].astype(BF16), win_ref[...])
    q = proj[:, 0:D_MODEL]
    f = proj[:, D_MODEL:2 * D_MODEL]
    v = proj[:, 2 * D_MODEL:3 * D_MODEL]
    gt = proj[:, 3 * D_MODEL:4 * D_MODEL]
    lb = lb_ref[...]
    a1 = jnp.log(lb)
    a2 = jnp.log1p(-lb) + _log_sigmoid(f)
    log_f = jnp.maximum(a1, a2) + _log1p_exp_neg(jnp.abs(a1 - a2))
    k = (1.0 - lb) * _sigmoid(-f)
    q = q * _sigmoid(q)
    gate = gt * _sigmoid(gt)
    bcum = _chunk_cumsum(tri_ref[...], log_f) * LOG2E

    row = lax.broadcasted_iota(jnp.int32, (CHUNK, LANES), 0)
    rowi = lax.broadcasted_iota(jnp.int32, (CHUNK, CHUNK), 0)
    coli = lax.broadcasted_iota(jnp.int32, (CHUNK, CHUNK), 1)
    row8 = lax.broadcasted_iota(jnp.int32, (SUB, CHUNK), 0)
    lane8 = lax.broadcasted_iota(jnp.int32, (SUB, CHUNK), 1)
    place = [[((lane8 == blk * SUB + s) & (row8 >= s)).astype(F32) for s in range(SUB)]
             for blk in range(CHUNK // SUB)]
    same_block = {}
    m = SUB
    while m < CHUNK:
        shift = int(math.log2(2 * m))
        same_block[m] = ((rowi >> shift) == (coli >> shift)).astype(F32)
        m *= 2
    nw = nw_ref[...]

    def head_scores(h, rows):
        qc = q_s[h, rows, :]
        kc = k_s[h, rows, :]
        bc = b_s[h, rows, :]
        att = jnp.zeros((CHUNK, CHUNK), F32)
        m = SUB
        while m < CHUNK:
            ref = jnp.concatenate(
                [jnp.broadcast_to(bc[blk * 2 * m + m - 1:blk * 2 * m + m, :], (2 * m, LANES))
                 for blk in range(CHUNK // (2 * m))], axis=0)
            upper = (row & m) != 0
            qh = qc * jnp.exp2(jnp.where(upper, bc - ref, NEG_INF))
            kh = kc * jnp.exp2(jnp.where(upper, NEG_INF, ref - bc))
            pair = _dot_nt(qh.astype(BF16), kh.astype(BF16))
            att = att + pair * same_block[m]
            m *= 2
        tiles = []
        for blk in range(CHUNK // SUB):
            sl = slice(blk * SUB, (blk + 1) * SUB)
            qi, bi = qc[sl], bc[sl]
            tile = jnp.zeros((SUB, CHUNK), F32)
            for s in range(SUB):
                r = rows.start + blk * SUB + s
                ks = jnp.broadcast_to(k_s[h, r:r + 1, :], (SUB, LANES))
                bs = jnp.broadcast_to(b_s[h, r:r + 1, :], (SUB, LANES))
                decay = jnp.exp2(jnp.minimum(bi - bs, 0.0))
                col = jnp.sum(qi * (ks * decay), axis=1, keepdims=True)
                tile = tile + col * place[blk][s]
            tiles.append(tile)
        return (att + jnp.concatenate(tiles, axis=0)).astype(BF16)

    def head_output(h, rows, att):
        st = state_s[h]
        qc = q_s[h, rows, :]
        kc = k_s[h, rows, :]
        vcb = v_s[h, rows, :].astype(BF16)
        bc = b_s[h, rows, :]
        b_last = bc[CHUNK - 1:CHUNK, :]
        o_s[h, rows, :] = _dot(att, vcb) + _dot_nt((qc * jnp.exp2(bc)).astype(BF16), st.astype(BF16))
        k_dec = kc * jnp.exp2(b_last - bc)
        state_s[h] = st * jnp.exp2(b_last) + _dot_tn(vcb, k_dec.astype(BF16))

    n_units = 2 * nc * HGRN_HEADS
    unit = 0
    for c in range(nc):
        rows = slice(c * CHUNK, (c + 1) * CHUNK)
        atts = []
        for h in range(HGRN_HEADS):
            atts.append(head_scores(h, rows))
            unit += 1
            ffn.advance_to(unit / n_units)
        for h in range(HGRN_HEADS):
            head_output(h, rows, atts[h])
            unit += 1
            ffn.advance_to(unit / n_units)
    o_ref[...] = ffn.finish(g2_ref, b2_ref)

    normed = []
    for h in range(HGRN_HEADS):
        o = o_s[h]
        ms = jnp.mean(o * o, axis=-1, keepdims=True)
        normed.append((o * lax.rsqrt(ms + HEAD_NORM_EPS) * nw * gate_s[h]).astype(BF16))
    y = _dot(jnp.concatenate(normed, axis=1), wout_ref[...])
    x1_new = _layernorm(ALPHA * xb_ref[...] + y, g_ref[...], b_ref[...])

    x1_s[...] = x1_new
    for h in range(HGRN_HEADS):
        sl = slice(h * HGRN_DK, (h + 1) * HGRN_DK)
        q_s[h] = q[:, sl]
        k_s[h] = k[:, sl]
        v_s[h] = v[:, sl]
        b_s[h] = bcum[:, sl]
        gate_s[h] = gate[:, sl]


def _chunk_tri(tb):
    i = jnp.arange(tb)
    same = (i[:, None] // CHUNK) == (i[None, :] // CHUNK)
    return (same & (i[None, :] <= i[:, None])).astype(BF16)


def _hgrn_layer(x, w_in, norm_w, w_out, layer, lb, g, b, w_gate_up, w_down, ffn_layer, g2, b2, tb):
    seq = x.shape[0]
    nb = seq // tb
    head_scratch = pltpu.VMEM((HGRN_HEADS, tb, HGRN_DK), F32)
    spec0, spec1, spec2 = _pipeline_specs(tb, D_MODEL, nb, 3)
    vec = _const_spec((1, D_MODEL))
    return pl.pallas_call(
        functools.partial(_hgrn_kernel, tb=tb),
        out_shape=jax.ShapeDtypeStruct((seq, D_MODEL), F32),
        grid=(nb + 2,),
        in_specs=[spec0, spec1, _layer_spec(w_in.shape, layer), vec, _const_spec((1, HGRN_DV)),
                  _layer_spec(w_out.shape, layer), vec, vec, _const_spec((tb, tb)),
                  _layer_spec(w_gate_up.shape, ffn_layer), _layer_spec(w_down.shape, ffn_layer), vec, vec],
        out_specs=spec2,
        scratch_shapes=[head_scratch] * 6 + [pltpu.VMEM((HGRN_HEADS, HGRN_DV, HGRN_DK), F32),
                                             pltpu.VMEM((tb, D_MODEL), F32)],
        compiler_params=_params(),
        name="hgrn_layer",
    )(x, x, w_in, lb, norm_w, w_out, g, b, _chunk_tri(tb), w_gate_up, w_down, g2, b2)


def _mlstm_kernel(xf_ref, xb_ref, win_ref, wg_ref, conv_ref, gb_ref, nw_ref, wout_ref, g_ref, b_ref, tri_ref, sel_ref,
                  wgu_ref, wd_ref, g2_ref, b2_ref, o_ref,
                  pad_s, q_s, k_s, v_s, crow_s, gcol_s, wi_s, em_s, ws_s, wp_s, sigo_s, h_s, c_s, m_s, x1_s, *, tb):
    nc = tb // CHUNK
    halo = SUB

    @pl.when(pl.program_id(0) == 0)
    def _():
        pad_s[0:halo, :] = jnp.zeros((halo, 2 * MLSTM_QK_W), F32)
        c_s[...] = jnp.zeros_like(c_s)
        m_s[...] = jnp.zeros_like(m_s)
        for ref in (q_s, k_s, v_s, crow_s, gcol_s, wi_s, ws_s, wp_s, sigo_s, x1_s):
            ref[...] = jnp.zeros_like(ref)
        em_s[...] = jnp.ones_like(em_s)

    ffn = _FfnPieces(x1_s[...], wgu_ref, wd_ref)
    xb = xf_ref[...].astype(BF16)
    proj = _dot(xb, win_ref[...])
    v_new = proj[:, 2 * MLSTM_QK_W:2 * MLSTM_QK_W + D_MODEL]
    o_pre = proj[:, 2 * MLSTM_QK_W + D_MODEL:]
    pad_s[halo:halo + tb, :] = proj[:, :2 * MLSTM_QK_W]
    conv_w = conv_ref[...]
    qk = pad_s[pl.ds(halo - MLSTM_CONV + 1, tb), :] * conv_w[0:1, :]
    for j in range(1, MLSTM_CONV):
        qk = qk + pad_s[pl.ds(halo - MLSTM_CONV + 1 + j, tb), :] * conv_w[j:j + 1, :]
    pad_s[0:halo, :] = pad_s[tb:tb + halo, :]
    qk = qk * _sigmoid(qk)
    q_new = qk[:, :MLSTM_QK_W]
    k_new = qk[:, MLSTM_QK_W:] * (MLSTM_DQK ** -0.5)
    gates = _dot(xb, wg_ref[...]) + gb_ref[...]
    li = gates[:, :LANES]
    bcum = _chunk_cumsum(tri_ref[...], _log_sigmoid(gates[:, LANES:]))
    cq = li - bcum
    rin = lax.broadcasted_iota(jnp.int32, (tb, LANES), 0) & (CHUNK - 1)
    cm = cq
    k = 1
    while k < CHUNK:
        shifted = pltpu.roll(cm, k, 0)
        cm = jnp.maximum(cm, jnp.where(rin >= k, shifted, NEG_INF))
        k *= 2
    m_prev = m_s[...]
    g_parts, wi_parts, em_parts, ws_parts, wp_parts, crow_new = [], [], [], [], [], []
    for c in range(nc):
        sl = slice(c * CHUNK, (c + 1) * CHUNK)
        mx = jnp.maximum(m_prev, cm[(c + 1) * CHUNK - 1:(c + 1) * CHUNK])
        g_c = jnp.maximum(cm[sl], m_prev)
        g_parts.append(g_c)
        wi_parts.append(jnp.exp(m_prev - g_c))
        em_parts.append(jnp.minimum(jnp.exp(-(bcum[sl] + g_c)), BF16_MAX))
        ws_parts.append(jnp.exp(cq[sl] - mx))
        wp_parts.append(jnp.exp(m_prev - mx))
        m_prev = bcum[(c + 1) * CHUNK - 1:(c + 1) * CHUNK] + mx
        crow_new.append(cq[sl].T[0:SUB, :])
    m_s[...] = m_prev
    wp_parts.append(jnp.zeros((SUB - nc, LANES), F32))
    def spread_lanes(tile, terms):
        parts, rest = [], tile
        for _ in range(terms):
            part = rest.astype(BF16)
            parts.append(part)
            rest = rest - part.astype(F32)
        return _dot(jnp.concatenate(parts, axis=1), sel_ref[0:terms * LANES, :])

    g_spread = spread_lanes(jnp.concatenate(g_parts, axis=0), 3)
    w_spread = spread_lanes(jnp.concatenate(wi_parts + em_parts + ws_parts + wp_parts, axis=0), 2)
    sig_o = _sigmoid(o_pre)

    lane_c = lax.broadcasted_iota(jnp.int32, (CHUNK, LANES), 1)
    rowi = lax.broadcasted_iota(jnp.int32, (CHUNK, CHUNK), 0)
    coli = lax.broadcasted_iota(jnp.int32, (CHUNK, CHUNK), 1)
    causal = rowi >= coli
    first_head = lane_c < MLSTM_DQK
    first_rows = lax.broadcasted_iota(jnp.int32, (2 * MLSTM_DQK, 2 * MLSTM_DV), 0) < MLSTM_DQK
    ones_blk = jnp.ones((CHUNK, LANES), BF16)
    nw_all = nw_ref[...]

    n_pairs = MLSTM_HEADS // 2
    n_units = 2 * nc * n_pairs
    unit = 0
    for c in range(nc):
        rows = slice(c * CHUNK, (c + 1) * CHUNK)
        crow = crow_s[c]
        wp_row = wp_s[c:c + 1, :]
        staged = []
        for p in range(n_pairs):
            lanes_p = slice(p * LANES, (p + 1) * LANES)
            cext = c_s[p]
            q_pair = q_s[rows, lanes_p]
            k_pair = k_s[rows, lanes_p]
            hsl = [slice((2 * p + j) * MLSTM_DV, (2 * p + j + 1) * MLSTM_DV) for j in range(2)]
            qm = jnp.concatenate([jnp.where(first_head, q_pair, 0.0), jnp.where(first_head, 0.0, q_pair)],
                                 axis=0).astype(BF16)
            v_ext = [jnp.concatenate([v_s[rows, hsl[j]].astype(BF16), ones_blk], axis=1) for j in range(2)]
            scores = _dot_nt(qm, k_pair.astype(BF16))
            inter = _dot(qm, cext.astype(BF16))
            kw = jnp.concatenate([jnp.where(first_head, k_pair * ws_s[rows, hsl[0]], 0.0),
                                  jnp.where(first_head, 0.0, k_pair * ws_s[rows, hsl[1]])], axis=0).astype(BF16)
            update = _dot_tn(kw, jnp.concatenate(v_ext, axis=0))
            w_prev = jnp.where(first_rows, jnp.tile(wp_row[:, hsl[0]], (1, 2)), jnp.tile(wp_row[:, hsl[1]], (1, 2)))
            c_s[p] = w_prev * cext + update
            staged.append((hsl, v_ext, scores, inter))
            unit += 1
            ffn.advance_to(unit / n_units)
        for p in range(n_pairs):
            hsl, v_ext, scores, inter = staged[p]
            for j in range(2):
                h = 2 * p + j
                hr = slice(j * CHUNK, (j + 1) * CHUNK)
                d_mat = jnp.exp(jnp.where(causal, crow[h:h + 1, :] - gcol_s[rows, h * MLSTM_DV:h * MLSTM_DV + CHUNK],
                                          NEG_INF))
                s = scores[hr] * d_mat
                both = _dot(s.astype(BF16), v_ext[j]) + jnp.tile(wi_s[rows, hsl[j]], (1, 2)) * inter[hr]
                floor = em_s[rows, hsl[j]]
                hh = both[:, :MLSTM_DV] / jnp.maximum(jnp.abs(both[:, MLSTM_DV:]), floor)
                h_s[rows, hsl[j]] = jnp.where(floor >= BF16_MAX, 0.0, hh)
            unit += 1
            ffn.advance_to(unit / n_units)
    o_ref[...] = ffn.finish(g2_ref, b2_ref)

    normed = []
    for h in range(MLSTM_HEADS):
        hsl = slice(h * MLSTM_DV, (h + 1) * MLSTM_DV)
        hh = h_s[:, hsl]
        ms = jnp.mean(hh * hh, axis=-1, keepdims=True)
        normed.append((hh * lax.rsqrt(ms + HEAD_NORM_EPS) * nw_all[:, hsl] * sigo_s[:, hsl]).astype(BF16))
    y = _dot(jnp.concatenate(normed, axis=1), wout_ref[...])
    x1_new = _layernorm(ALPHA * xb_ref[...] + y, g_ref[...], b_ref[...])

    x1_s[...] = x1_new
    q_s[...] = q_new
    k_s[...] = k_new
    v_s[...] = v_new
    sigo_s[...] = sig_o
    for c in range(nc):
        crow_s[c] = crow_new[c]
    gcol_s[...] = g_spread
    wi_s[...] = w_spread[0:tb]
    em_s[...] = w_spread[tb:2 * tb]
    ws_s[...] = w_spread[2 * tb:3 * tb]
    wp_s[...] = w_spread[3 * tb:3 * tb + SUB]


MLSTM_MAIN_W = 2 * MLSTM_QK_W + 2 * D_MODEL


def _mlstm_gate_weights(w_in, gate_b):
    pad = LANES - MLSTM_HEADS
    w_i = jnp.pad(w_in[:, MLSTM_MAIN_W:MLSTM_MAIN_W + MLSTM_HEADS], ((0, 0), (0, pad)))
    w_f = jnp.pad(w_in[:, MLSTM_MAIN_W + MLSTM_HEADS:], ((0, 0), (0, pad)))
    b_i = jnp.pad(gate_b[:MLSTM_HEADS], (0, pad))
    b_f = jnp.pad(gate_b[MLSTM_HEADS:], (0, pad))
    return jnp.concatenate([w_i, w_f], axis=1).astype(BF16), jnp.concatenate([b_i, b_f]).reshape(1, 2 * LANES)


def _lane_spread_matrix():
    src = jnp.arange(LANES)[:, None]
    dst = jnp.arange(D_MODEL)[None, :] // LANES
    return jnp.tile((src == dst).astype(BF16), (3, 1))


def _mlstm_layer(x, w_in, w_gate, conv_w, gate_b, norm_w, w_out, layer, g, b,
                 w_gate_up, w_down, ffn_layer, g2, b2, tb):
    seq = x.shape[0]
    nb = seq // tb
    assert tb // CHUNK <= SUB
    wide = pltpu.VMEM((tb, D_MODEL), F32)
    sel = _lane_spread_matrix()
    spec0, spec1, spec2 = _pipeline_specs(tb, D_MODEL, nb, 3)
    vec = _const_spec((1, D_MODEL))
    return pl.pallas_call(
        functools.partial(_mlstm_kernel, tb=tb),
        out_shape=jax.ShapeDtypeStruct((seq, D_MODEL), F32),
        grid=(nb + 2,),
        in_specs=[spec0, spec1, _layer_spec(w_in.shape, layer, MLSTM_MAIN_W), _const_spec(w_gate.shape),
                  _const_spec(conv_w.shape), _const_spec((1, 2 * LANES)), vec,
                  _layer_spec(w_out.shape, layer), vec, vec, _const_spec((tb, tb)), _const_spec(sel.shape),
                  _layer_spec(w_gate_up.shape, ffn_layer), _layer_spec(w_down.shape, ffn_layer), vec, vec],
        out_specs=spec2,
        scratch_shapes=[pltpu.VMEM((tb + SUB, 2 * MLSTM_QK_W), F32),
                        pltpu.VMEM((tb, MLSTM_QK_W), F32), pltpu.VMEM((tb, MLSTM_QK_W), F32),
                        wide, pltpu.VMEM((tb // CHUNK, SUB, CHUNK), F32), wide, wide, wide, wide,
                        pltpu.VMEM((SUB, D_MODEL), F32), wide, wide,
                        pltpu.VMEM((MLSTM_HEADS // 2, 2 * MLSTM_DQK, 2 * MLSTM_DV), F32),
                        pltpu.VMEM((1, LANES), F32), wide],
        compiler_params=_params(),
        name="mlstm_layer",
    )(x, x, w_in, w_gate, conv_w, gate_b, norm_w, w_out, g, b, _chunk_tri(tb), sel, w_gate_up, w_down, g2, b2)


S5_LANE_GROUPS = LANES // S5_GROUP_CH
S5_BLOCKS = D_MODEL // LANES
S5_BLOCK_STATE = S5_LANE_GROUPS * S5_STATE
S5_TAPS = 4


def _s5_kernel(x_ref, win_ref, bre_ref, bim_ref, cre_ref, cim_ref, are_ref, aim_ref, d_ref, wout_ref,
               g_ref, b_ref, wgu_ref, wd_ref, g2_ref, b2_ref, o_ref, hr_s, hi_s, x1_s, *, tb):
    @pl.when(pl.program_id(0) == 0)
    def _():
        hr_s[...] = jnp.zeros_like(hr_s)
        hi_s[...] = jnp.zeros_like(hi_s)
        x1_s[...] = jnp.zeros_like(x1_s)

    ffn = _FfnPieces(x1_s[...], wgu_ref, wd_ref)
    x = x_ref[...]
    u = _dot(x.astype(BF16), win_ref[...])
    ub = u.astype(BF16)
    nt = tb // SUB
    row8 = lax.broadcasted_iota(jnp.int32, (SUB, S5_BLOCK_STATE), 0)

    def tile_scan(zr, zi, mults, first=0):
        for i in range(first, len(mults)):
            mr, mi = mults[i]
            rr = pltpu.roll(zr, 1 << i, 1)
            ri = pltpu.roll(zi, 1 << i, 1)
            zr, zi = zr + (mr * rr - mi * ri), zi + (mr * ri + mi * rr)
        return zr, zi

    row_u = lax.broadcasted_iota(jnp.int32, (tb, D_MODEL), 0) & (SUB - 1)
    u_taps = [ub] + [jnp.where(row_u >= k, pltpu.roll(u, k, 0), 0.0).astype(BF16) for k in range(1, S5_TAPS)]

    def input_states(j):
        lanes = slice(j * LANES, (j + 1) * LANES)
        uj = jnp.concatenate([t[:, lanes] for t in u_taps], axis=1)
        return (_dot(uj, bre_ref[j]).reshape(nt, SUB, S5_BLOCK_STATE),
                _dot(uj, bim_ref[j]).reshape(nt, SUB, S5_BLOCK_STATE))

    y_blocks = []
    for j in range(S5_BLOCKS):
        st = slice(j * S5_BLOCK_STATE, (j + 1) * S5_BLOCK_STATE)
        ar = are_ref[:, st]
        ai = aim_ref[:, st]
        mults = []
        pr, pi = ar, ai
        for i in range(3):
            keep = row8 >= (1 << i)
            mults.append((jnp.where(keep, pr, 0.0), jnp.where(keep, pi, 0.0)))
            pr, pi = pr * pr - pi * pi, 2.0 * pr * pi
        pwr, pwi = tile_scan(jnp.where(row8 == 0, ar, 0.0)[None], jnp.where(row8 == 0, ai, 0.0)[None], mults)
        pwr, pwi = pwr[0], pwi[0]
        sr, si = input_states(j)
        ffn.advance_to((j + 1) / S5_BLOCKS)
        sr, si = tile_scan(sr, si, mults, first=int(math.log2(S5_TAPS)))
        cr = hr_s[:, st]
        ci = hi_s[:, st]
        tiles_r, tiles_i = [], []
        for n in range(nt):
            tr = sr[n] + (pwr * cr - pwi * ci)
            ti = si[n] + (pwr * ci + pwi * cr)
            tiles_r.append(tr)
            tiles_i.append(ti)
            cr, ci = tr[SUB - 1:SUB, :], ti[SUB - 1:SUB, :]
        hr_s[:, st] = cr
        hi_s[:, st] = ci
        hr = jnp.concatenate(tiles_r, axis=0)
        hi = jnp.concatenate(tiles_i, axis=0)
        y_blocks.append(_dot(hr.astype(BF16), cre_ref[j]) - _dot(hi.astype(BF16), cim_ref[j]))
    o_ref[...] = ffn.finish(g2_ref, b2_ref)

    y = jax.nn.gelu(jnp.concatenate(y_blocks, axis=1) + d_ref[...] * u)
    z = _dot(y.astype(BF16), wout_ref[...])
    mix = z[:, :D_MODEL] * _sigmoid(z[:, D_MODEL:])
    x1_s[...] = _layernorm(ALPHA * x + mix, g_ref[...], b_ref[...])


def _s5_block_diag(t):
    a, b = t.shape[1], t.shape[2]
    t = t.reshape(S5_BLOCKS, S5_LANE_GROUPS, a, b)
    eye = jnp.eye(S5_LANE_GROUPS, dtype=t.dtype)
    out = jnp.einsum("jgab,gh->jgahb", t, eye)
    return out.reshape(S5_BLOCKS, S5_LANE_GROUPS * a, S5_LANE_GROUPS * b)


def _s5_layer(x, w_in, a_re, a_im, log_dt, b_re, b_im, c_re, c_im, d_skip, w_out, layer, g, b,
              w_gate_up, w_down, ffn_layer, g2, b2, tb):
    seq = x.shape[0]
    nb = seq // tb
    dt = jnp.exp(log_dt)[:, None]
    mag = jnp.exp(a_re * dt)
    abar_re = mag * jnp.cos(a_im * dt)
    abar_im = mag * jnp.sin(a_im * dt)
    nr = abar_re - 1.0
    ni = abar_im
    den = a_re * a_re + a_im * a_im
    coef_re = (nr * a_re + ni * a_im) / den
    coef_im = (ni * a_re - nr * a_im) / den
    bbar_re = coef_re[..., None] * b_re - coef_im[..., None] * b_im
    bbar_im = coef_re[..., None] * b_im + coef_im[..., None] * b_re
    taps_re, taps_im = [bbar_re], [bbar_im]
    for _ in range(1, S5_TAPS):
        pr, pi = taps_re[-1], taps_im[-1]
        taps_re.append(abar_re[..., None] * pr - abar_im[..., None] * pi)
        taps_im.append(abar_re[..., None] * pi + abar_im[..., None] * pr)
    stack_taps = lambda taps: jnp.concatenate(
        [_s5_block_diag(jnp.swapaxes(t, 1, 2)) for t in taps], axis=1).astype(BF16)
    bre = stack_taps(taps_re)
    bim = stack_taps(taps_im)
    cre = _s5_block_diag(jnp.swapaxes(c_re, 1, 2)).astype(BF16)
    cim = _s5_block_diag(jnp.swapaxes(c_im, 1, 2)).astype(BF16)
    n_state = S5_GROUPS * S5_STATE
    spec0, spec1 = _pipeline_specs(tb, D_MODEL, nb, 2)
    vec = _const_spec((1, D_MODEL))
    return pl.pallas_call(
        functools.partial(_s5_kernel, tb=tb),
        out_shape=jax.ShapeDtypeStruct((seq, D_MODEL), F32),
        grid=(nb + 1,),
        in_specs=[spec0, _layer_spec(w_in.shape, layer), _const_spec(bre.shape),
                  _const_spec(bim.shape), _const_spec(cre.shape), _const_spec(cim.shape), _const_spec((1, n_state)),
                  _const_spec((1, n_state)), vec, _layer_spec(w_out.shape, layer), vec, vec,
                  _layer_spec(w_gate_up.shape, ffn_layer), _layer_spec(w_down.shape, ffn_layer), vec, vec],
        out_specs=spec1,
        scratch_shapes=[pltpu.VMEM((1, n_state), F32), pltpu.VMEM((1, n_state), F32),
                        pltpu.VMEM((tb, D_MODEL), F32)],
        compiler_params=_params(),
        name="s5_layer",
    )(x, w_in, bre, bim, cre, cim, abar_re.reshape(1, n_state), abar_im.reshape(1, n_state),
      d_skip.reshape(1, D_MODEL), w_out, g, b, w_gate_up, w_down, g2, b2)


def _hgrn_lower_bounds(lb_logits):
    p = jax.nn.softmax(lb_logits.astype(F32), axis=0)
    c = jnp.cumsum(p, axis=0)
    return c - c[0:1]


def kernel(x, hgrn_w_in, hgrn_norm_w, hgrn_w_out, hgrn_lb_logits, mlstm_w_in, mlstm_conv_w, mlstm_gate_b, mlstm_norm_w, mlstm_w_out, s5_w_in, s5_a_re, s5_a_im, s5_log_dt, s5_b_re, s5_b_im, s5_c_re, s5_c_im, s5_d, s5_w_out, ffn_w_gate_up, ffn_w_down, ln_g, ln_b):
    bsz, seq, _ = x.shape
    tb = math.gcd(seq, ROW_BLOCK)
    assert tb % CHUNK == 0, "sequence length must be a multiple of the recurrence chunk"
    lb_all = _hgrn_lower_bounds(hgrn_lb_logits)
    row = lambda t: t.reshape(1, -1)
    (hgrn_w_in, hgrn_w_out, mlstm_w_in_b, mlstm_w_out, s5_w_in, s5_w_out, ffn_w_gate_up, ffn_w_down) = (
        w.astype(BF16) for w in (hgrn_w_in, hgrn_w_out, mlstm_w_in, mlstm_w_out, s5_w_in, s5_w_out,
                                 ffn_w_gate_up, ffn_w_down))
    outs = []
    for bi in range(bsz):
        h = x[bi]
        for i in range(DEPTH):
            kind = i % N_MIXERS
            j = i // N_MIXERS
            g0, b0 = row(ln_g[i, 0]), row(ln_b[i, 0])
            ffn_args = (ffn_w_gate_up, ffn_w_down, i, row(ln_g[i, 1]), row(ln_b[i, 1]), tb)
            if kind == 0:
                h = _hgrn_layer(h, hgrn_w_in, row(hgrn_norm_w[j]), hgrn_w_out, j, row(lb_all[i]), g0, b0, *ffn_args)
            elif kind == 1:
                w_gate, gate_b = _mlstm_gate_weights(mlstm_w_in[j], mlstm_gate_b[j])
                h = _mlstm_layer(h, mlstm_w_in_b, w_gate, mlstm_conv_w[j], gate_b, row(mlstm_norm_w[j]),
                                 mlstm_w_out, j, g0, b0, *ffn_args)
            else:
                h = _s5_layer(h, s5_w_in, s5_a_re[j], s5_a_im[j], s5_log_dt[j], s5_b_re[j], s5_b_im[j],
                              s5_c_re[j], s5_c_im[j], s5_d[j], s5_w_out, j, g0, b0, *ffn_args)
        outs.append(h)
    return jnp.stack(outs, axis=0)
```

```python
import functools
import math

import jax
import jax.numpy as jnp
from jax import lax
from jax.experimental import pallas as pl
from jax.experimental.pallas import tpu as pltpu

F32 = jnp.float32
BF16 = jnp.bfloat16

D_MODEL = 1024
DEPTH = 4
N_MIXERS = 3
ALPHA = (2.0 * DEPTH) ** 0.25
LN_EPS = 1e-5
HEAD_NORM_EPS = 1e-6

HGRN_HEADS = 8
HGRN_DK = 128
HGRN_DV = 128

MLSTM_HEADS = 8
MLSTM_DV = 128
MLSTM_DQK = 64
MLSTM_CONV = 4
MLSTM_QK_W = MLSTM_HEADS * MLSTM_DQK

S5_GROUP_CH = 16
S5_GROUPS = D_MODEL // S5_GROUP_CH
S5_STATE = 64

FFN_HIDDEN = -(-8 * D_MODEL // (3 * 256)) * 256

CHUNK = 128
SUB = 8
LANES = 128
ROW_BLOCK = 256
VMEM_LIMIT_BYTES = 60 * 1024 * 1024

NEG_INF = float("-inf")
BF16_MAX = float(jnp.finfo(jnp.bfloat16).max)


def _dot(a, b):
    return jnp.dot(a, b, preferred_element_type=F32)


def _dot_nt(a, b):
    return lax.dot_general(a, b, (((1,), (1,)), ((), ())), preferred_element_type=F32)


def _dot_tn(a, b):
    return lax.dot_general(a, b, (((0,), (0,)), ((), ())), preferred_element_type=F32)


def _layernorm(y, g, b):
    mu = jnp.mean(y, axis=-1, keepdims=True)
    yc = y - mu
    var = jnp.mean(yc * yc, axis=-1, keepdims=True)
    return yc * lax.rsqrt(var + LN_EPS) * g + b


LOG2E = 1.4426950408889634


def _log1p_exp_neg(z):
    return jnp.log(1.0 + jnp.exp2(z * -LOG2E))


def _log_sigmoid(z):
    return jnp.minimum(z, 0.0) - _log1p_exp_neg(jnp.abs(z))


def _sigmoid(z):
    return 1.0 / (1.0 + jnp.exp2(z * -LOG2E))


def _chunk_cumsum(tri, z):
    hi = z.astype(BF16)
    r1 = z - hi.astype(F32)
    mid = r1.astype(BF16)
    lo = (r1 - mid.astype(F32)).astype(BF16)
    return _dot(tri, hi) + _dot(tri, mid) + _dot(tri, lo)


def _const_spec(shape):
    nd = len(shape)
    return pl.BlockSpec(shape, lambda i: (0,) * nd, pipeline_mode=pl.Buffered(1))


def _layer_spec(stacked_shape, layer, cols=None):
    _, rows, width = stacked_shape
    return pl.BlockSpec((None, rows, cols or width), lambda i: (layer, 0, 0), pipeline_mode=pl.Buffered(1))


def _pipeline_specs(tb, width, nb, stages):
    return [pl.BlockSpec((tb, width), functools.partial(lambda i, s: (jnp.clip(i - s, 0, nb - 1), 0), s=s))
            for s in range(stages)]


def _params():
    return pltpu.CompilerParams(dimension_semantics=("arbitrary",), vmem_limit_bytes=VMEM_LIMIT_BYTES)


FFN_PIECE = 256


class _FfnPieces:
    def __init__(self, x, wgu_ref, wd_ref):
        self.x = x
        self.xb = x.astype(BF16)
        self.wgu_ref = wgu_ref
        self.wd_ref = wd_ref
        self.n_up = FFN_HIDDEN // FFN_PIECE
        self.n_down = D_MODEL // FFN_PIECE
        self.done = 0
        self.act = []
        self.out = []

    def advance_to(self, fraction):
        total = self.n_up + self.n_down
        target = min(total, int(fraction * total + 1e-9))
        while self.done < target:
            if self.done < self.n_up:
                lo = self.done * FFN_PIECE
                gate = _dot(self.xb, self.wgu_ref[:, lo:lo + FFN_PIECE])
                up = _dot(self.xb, self.wgu_ref[:, FFN_HIDDEN + lo:FFN_HIDDEN + lo + FFN_PIECE])
                self.act.append((gate * _sigmoid(gate) * up).astype(BF16))
            else:
                if len(self.act) > 1:
                    self.act = [jnp.concatenate(self.act, axis=1)]
                lo = (self.done - self.n_up) * FFN_PIECE
                self.out.append(_dot(self.act[0], self.wd_ref[:, lo:lo + FFN_PIECE]))
            self.done += 1

    def finish(self, g_ref, b_ref):
        self.advance_to(1.0)
        return _layernorm(ALPHA * self.x + jnp.concatenate(self.out, axis=1), g_ref[...], b_ref[...])


def _hgrn_kernel(xf_ref, xb_ref, win_ref, lb_ref, nw_ref, wout_ref, g_ref, b_ref, tri_ref,
                 wgu_ref, wd_ref, g2_ref, b2_ref, o_ref,
                 q_s, k_s, v_s, b_s, gate_s, o_s, state_s, x1_s, *, tb):
    nc = tb // CHUNK

    @pl.when(pl.program_id(0) == 0)
    def _():
        state_s[...] = jnp.zeros_like(state_s)
        for ref in (q_s, k_s, v_s, b_s, gate_s, x1_s):
            ref[...] = jnp.zeros_like(ref)

    ffn = _FfnPieces(x1_s[...], wgu_ref, wd_ref)

    proj = _dot(xf_ref[...].astype(BF16), win_ref[...])
    q = proj[:, 0:D_MODEL]
    f = proj[:, D_MODEL:2 * D_MODEL]
    v = proj[:, 2 * D_MODEL:3 * D_MODEL]
    gt = proj[:, 3 * D_MODEL:4 * D_MODEL]
    lb = lb_ref[...]
    a1 = jnp.log(lb)
    a2 = jnp.log1p(-lb) + _log_sigmoid(f)
    log_f = jnp.maximum(a1, a2) + _log1p_exp_neg(jnp.abs(a1 - a2))
    k = (1.0 - lb) * _sigmoid(-f)
    q = q * _sigmoid(q)
    gate = gt * _sigmoid(gt)
    bcum = _chunk_cumsum(tri_ref[...], log_f) * LOG2E

    row = lax.broadcasted_iota(jnp.int32, (CHUNK, LANES), 0)
    rowi = lax.broadcasted_iota(jnp.int32, (CHUNK, CHUNK), 0)
    coli = lax.broadcasted_iota(jnp.int32, (CHUNK, CHUNK), 1)
    row8 = lax.broadcasted_iota(jnp.int32, (SUB, CHUNK), 0)
    lane8 = lax.broadcasted_iota(jnp.int32, (SUB, CHUNK), 1)
    place = [[((lane8 == blk * SUB + s) & (row8 >= s)).astype(F32) for s in range(SUB)]
             for blk in range(CHUNK // SUB)]
    same_block = {}
    m = SUB
    while m < CHUNK:
        shift = int(math.log2(2 * m))
        same_block[m] = ((rowi >> shift) == (coli >> shift)).astype(F32)
        m *= 2
    nw = nw_ref[...]

    def head_scores(h, rows):
        qc = q_s[h, rows, :]
        kc = k_s[h, rows, :]
        bc = b_s[h, rows, :]
        att = jnp.zeros((CHUNK, CHUNK), F32)
        m = SUB
        while m < CHUNK:
            ref = jnp.concatenate(
                [jnp.broadcast_to(bc[blk * 2 * m + m - 1:blk * 2 * m + m, :], (2 * m, LANES))
                 for blk in range(CHUNK // (2 * m))], axis=0)
            upper = (row & m) != 0
            qh = qc * jnp.exp2(jnp.where(upper, bc - ref, NEG_INF))
            kh = kc * jnp.exp2(jnp.where(upper, NEG_INF, ref - bc))
            pair = _dot_nt(qh.astype(BF16), kh.astype(BF16))
            att = att + pair * same_block[m]
            m *= 2
        tiles = []
        for blk in range(CHUNK // SUB):
            sl = slice(blk * SUB, (blk + 1) * SUB)
            qi, bi = qc[sl], bc[sl]
            tile = jnp.zeros((SUB, CHUNK), F32)
            for s in range(SUB):
                r = rows.start + blk * SUB + s
                ks = jnp.broadcast_to(k_s[h, r:r + 1, :], (SUB, LANES))
                bs = jnp.broadcast_to(b_s[h, r:r + 1, :], (SUB, LANES))
                decay = jnp.exp2(jnp.minimum(bi - bs, 0.0))
                col = jnp.sum(qi * (ks * decay), axis=1, keepdims=True)
                tile = tile + col * place[blk][s]
            tiles.append(tile)
        return (att + jnp.concatenate(tiles, axis=0)).astype(BF16)

    def head_output(h, rows, att):
        st = state_s[h]
        qc = q_s[h, rows, :]
        kc = k_s[h, rows, :]
        vcb = v_s[h, rows, :].astype(BF16)
        bc = b_s[h, rows, :]
        b_last = bc[CHUNK - 1:CHUNK, :]
        o_s[h, rows, :] = _dot(att, vcb) + _dot_nt((qc * jnp.exp2(bc)).astype(BF16), st.astype(BF16))
        k_dec = kc * jnp.exp2(b_last - bc)
        state_s[h] = st * jnp.exp2(b_last) + _dot_tn(vcb, k_dec.astype(BF16))

    n_units = 2 * nc * HGRN_HEADS
    unit = 0
    for c in range(nc):
        rows = slice(c * CHUNK, (c + 1) * CHUNK)
        atts = []
        for h in range(HGRN_HEADS):
            atts.append(head_scores(h, rows))
            unit += 1
            ffn.advance_to(unit / n_units)
        for h in range(HGRN_HEADS):
            head_output(h, rows, atts[h])
            unit += 1
            ffn.advance_to(unit / n_units)
    o_ref[...] = ffn.finish(g2_ref, b2_ref)

    normed = []
    for h in range(HGRN_HEADS):
        o = o_s[h]
        ms = jnp.mean(o * o, axis=-1, keepdims=True)
        normed.append((o * lax.rsqrt(ms + HEAD_NORM_EPS) * nw * gate_s[h]).astype(BF16))
    y = _dot(jnp.concatenate(normed, axis=1), wout_ref[...])
    x1_new = _layernorm(ALPHA * xb_ref[...] + y, g_ref[...], b_ref[...])

    x1_s[...] = x1_new
    for h in range(HGRN_HEADS):
        sl = slice(h * HGRN_DK, (h + 1) * HGRN_DK)
        q_s[h] = q[:, sl]
        k_s[h] = k[:, sl]
        v_s[h] = v[:, sl]
        b_s[h] = bcum[:, sl]
        gate_s[h] = gate[:, sl]


def _chunk_tri(tb):
    i = jnp.arange(tb)
    same = (i[:, None] // CHUNK) == (i[None, :] // CHUNK)
    return (same & (i[None, :] <= i[:, None])).astype(BF16)


def _hgrn_layer(x, w_in, norm_w, w_out, layer, lb, g, b, w_gate_up, w_down, ffn_layer, g2, b2, tb):
    seq = x.shape[0]
    nb = seq // tb
    head_scratch = pltpu.VMEM((HGRN_HEADS, tb, HGRN_DK), F32)
    spec0, spec1, spec2 = _pipeline_specs(tb, D_MODEL, nb, 3)
    vec = _const_spec((1, D_MODEL))
    return pl.pallas_call(
        functools.partial(_hgrn_kernel, tb=tb),
        out_shape=jax.ShapeDtypeStruct((seq, D_MODEL), F32),
        grid=(nb + 2,),
        in_specs=[spec0, spec1, _layer_spec(w_in.shape, layer), vec, _const_spec((1, HGRN_DV)),
                  _layer_spec(w_out.shape, layer), vec, vec, _const_spec((tb, tb)),
                  _layer_spec(w_gate_up.shape, ffn_layer), _layer_spec(w_down.shape, ffn_layer), vec, vec],
        out_specs=spec2,
        scratch_shapes=[head_scratch] * 6 + [pltpu.VMEM((HGRN_HEADS, HGRN_DV, HGRN_DK), F32),
                                             pltpu.VMEM((tb, D_MODEL), F32)],
        compiler_params=_params(),
        name="hgrn_layer",
    )(x, x, w_in, lb, norm_w, w_out, g, b, _chunk_tri(tb), w_gate_up, w_down, g2, b2)


def _mlstm_kernel(xf_ref, xb_ref, win_ref, wg_ref, conv_ref, gb_ref, nw_ref, wout_ref, g_ref, b_ref, tri_ref, sel_ref,
                  wgu_ref, wd_ref, g2_ref, b2_ref, o_ref,
                  pad_s, q_s, k_s, v_s, crow_s, gcol_s, wi_s, em_s, ws_s, wp_s, sigo_s, h_s, c_s, m_s, x1_s, *, tb):
    nc = tb // CHUNK
    halo = SUB

    @pl.when(pl.program_id(0) == 0)
    def _():
        pad_s[0:halo, :] = jnp.zeros((halo, 2 * MLSTM_QK_W), F32)
        c_s[...] = jnp.zeros_like(c_s)
        m_s[...] = jnp.zeros_like(m_s)
        for ref in (q_s, k_s, v_s, crow_s, gcol_s, wi_s, ws_s, wp_s, sigo_s, x1_s):
            ref[...] = jnp.zeros_like(ref)
        em_s[...] = jnp.ones_like(em_s)

    ffn = _FfnPieces(x1_s[...], wgu_ref, wd_ref)
    xb = xf_ref[...].astype(BF16)
    proj = _dot(xb, win_ref[...])
    v_new = proj[:, 2 * MLSTM_QK_W:2 * MLSTM_QK_W + D_MODEL]
    o_pre = proj[:, 2 * MLSTM_QK_W + D_MODEL:]
    pad_s[halo:halo + tb, :] = proj[:, :2 * MLSTM_QK_W]
    conv_w = conv_ref[...]
    qk = pad_s[pl.ds(halo - MLSTM_CONV + 1, tb), :] * conv_w[0:1, :]
    for j in range(1, MLSTM_CONV):
        qk = qk + pad_s[pl.ds(halo - MLSTM_CONV + 1 + j, tb), :] * conv_w[j:j + 1, :]
    pad_s[0:halo, :] = pad_s[tb:tb + halo, :]
    qk = qk * _sigmoid(qk)
    q_new = qk[:, :MLSTM_QK_W]
    k_new = qk[:, MLSTM_QK_W:] * (MLSTM_DQK ** -0.5)
    gates = _dot(xb, wg_ref[...]) + gb_ref[...]
    li = gates[:, :LANES]
    bcum = _chunk_cumsum(tri_ref[...], _log_sigmoid(gates[:, LANES:]))
    cq = li - bcum
    rin = lax.broadcasted_iota(jnp.int32, (tb, LANES), 0) & (CHUNK - 1)
    cm = cq
    k = 1
    while k < CHUNK:
        shifted = pltpu.roll(cm, k, 0)
        cm = jnp.maximum(cm, jnp.where(rin >= k, shifted, NEG_INF))
        k *= 2
    m_prev = m_s[...]
    g_parts, wi_parts, em_parts, ws_parts, wp_parts, crow_new = [], [], [], [], [], []
    for c in range(nc):
        sl = slice(c * CHUNK, (c + 1) * CHUNK)
        mx = jnp.maximum(m_prev, cm[(c + 1) * CHUNK - 1:(c + 1) * CHUNK])
        g_c = jnp.maximum(cm[sl], m_prev)
        g_parts.append(g_c)
        wi_parts.append(jnp.exp(m_prev - g_c))
        em_parts.append(jnp.minimum(jnp.exp(-(bcum[sl] + g_c)), BF16_MAX))
        ws_parts.append(jnp.exp(cq[sl] - mx))
        wp_parts.append(jnp.exp(m_prev - mx))
        m_prev = bcum[(c + 1) * CHUNK - 1:(c + 1) * CHUNK] + mx
        crow_new.append(cq[sl].T[0:SUB, :])
    m_s[...] = m_prev
    wp_parts.append(jnp.zeros((SUB - nc, LANES), F32))
    def spread_lanes(tile, terms):
        parts, rest = [], tile
        for _ in range(terms):
            part = rest.astype(BF16)
            parts.append(part)
            rest = rest - part.astype(F32)
        return _dot(jnp.concatenate(parts, axis=1), sel_ref[0:terms * LANES, :])

    g_spread = spread_lanes(jnp.concatenate(g_parts, axis=0), 3)
    w_spread = spread_lanes(jnp.concatenate(wi_parts + em_parts + ws_parts + wp_parts, axis=0), 2)
    sig_o = _sigmoid(o_pre)

    lane_c = lax.broadcasted_iota(jnp.int32, (CHUNK, LANES), 1)
    rowi = lax.broadcasted_iota(jnp.int32, (CHUNK, CHUNK), 0)
    coli = lax.broadcasted_iota(jnp.int32, (CHUNK, CHUNK), 1)
    causal = rowi >= coli
    first_head = lane_c < MLSTM_DQK
    first_rows = lax.broadcasted_iota(jnp.int32, (2 * MLSTM_DQK, 2 * MLSTM_DV), 0) < MLSTM_DQK
    ones_blk = jnp.ones((CHUNK, LANES), BF16)
    nw_all = nw_ref[...]

    n_pairs = MLSTM_HEADS // 2
    n_units = 2 * nc * n_pairs
    unit = 0
    for c in range(nc):
        rows = slice(c * CHUNK, (c + 1) * CHUNK)
        crow = crow_s[c]
        wp_row = wp_s[c:c + 1, :]
        staged = []
        for p in range(n_pairs):
            lanes_p = slice(p * LANES, (p + 1) * LANES)
            cext = c_s[p]
            q_pair = q_s[rows, lanes_p]
            k_pair = k_s[rows, lanes_p]
            hsl = [slice((2 * p + j) * MLSTM_DV, (2 * p + j + 1) * MLSTM_DV) for j in range(2)]
            qm = jnp.concatenate([jnp.where(first_head, q_pair, 0.0), jnp.where(first_head, 0.0, q_pair)],
                                 axis=0).astype(BF16)
            v_ext = [jnp.concatenate([v_s[rows, hsl[j]].astype(BF16), ones_blk], axis=1) for j in range(2)]
            scores = _dot_nt(qm, k_pair.astype(BF16))
            inter = _dot(qm, cext.astype(BF16))
            kw = jnp.concatenate([jnp.where(first_head, k_pair * ws_s[rows, hsl[0]], 0.0),
                                  jnp.where(first_head, 0.0, k_pair * ws_s[rows, hsl[1]])], axis=0).astype(BF16)
            update = _dot_tn(kw, jnp.concatenate(v_ext, axis=0))
            w_prev = jnp.where(first_rows, jnp.tile(wp_row[:, hsl[0]], (1, 2)), jnp.tile(wp_row[:, hsl[1]], (1, 2)))
            c_s[p] = w_prev * cext + update
            staged.append((hsl, v_ext, scores, inter))
            unit += 1
            ffn.advance_to(unit / n_units)
        for p in range(n_pairs):
            hsl, v_ext, scores, inter = staged[p]
            for j in range(2):
                h = 2 * p + j
                hr = slice(j * CHUNK, (j + 1) * CHUNK)
                d_mat = jnp.exp(jnp.where(causal, crow[h:h + 1, :] - gcol_s[rows, h * MLSTM_DV:h * MLSTM_DV + CHUNK],
                                          NEG_INF))
                s = scores[hr] * d_mat
                both = _dot(s.astype(BF16), v_ext[j]) + jnp.tile(wi_s[rows, hsl[j]], (1, 2)) * inter[hr]
                floor = em_s[rows, hsl[j]]
                hh = both[:, :MLSTM_DV] / jnp.maximum(jnp.abs(both[:, MLSTM_DV:]), floor)
                h_s[rows, hsl[j]] = jnp.where(floor >= BF16_MAX, 0.0, hh)
            unit += 1
            ffn.advance_to(unit / n_units)
    o_ref[...] = ffn.finish(g2_ref, b2_ref)

    normed = []
    for h in range(MLSTM_HEADS):
        hsl = slice(h * MLSTM_DV, (h + 1) * MLSTM_DV)
        hh = h_s[:, hsl]
        ms = jnp.mean(hh * hh, axis=-1, keepdims=True)
        normed.append((hh * lax.rsqrt(ms + HEAD_NORM_EPS) * nw_all[:, hsl] * sigo_s[:, hsl]).astype(BF16))
    y = _dot(jnp.concatenate(normed, axis=1), wout_ref[...])
    x1_new = _layernorm(ALPHA * xb_ref[...] + y, g_ref[...], b_ref[...])

    x1_s[...] = x1_new
    q_s[...] = q_new
    k_s[...] = k_new
    v_s[...] = v_new
    sigo_s[...] = sig_o
    for c in range(nc):
        crow_s[c] = crow_new[c]
    gcol_s[...] = g_spread
    wi_s[...] = w_spread[0:tb]
    em_s[...] = w_spread[tb:2 * tb]
    ws_s[...] = w_spread[2 * tb:3 * tb]
    wp_s[...] = w_spread[3 * tb:3 * tb + SUB]


MLSTM_MAIN_W = 2 * MLSTM_QK_W + 2 * D_MODEL


def _mlstm_gate_weights(w_in, gate_b):
    pad = LANES - MLSTM_HEADS
    w_i = jnp.pad(w_in[:, MLSTM_MAIN_W:MLSTM_MAIN_W + MLSTM_HEADS], ((0, 0), (0, pad)))
    w_f = jnp.pad(w_in[:, MLSTM_MAIN_W + MLSTM_HEADS:], ((0, 0), (0, pad)))
    b_i = jnp.pad(gate_b[:MLSTM_HEADS], (0, pad))
    b_f = jnp.pad(gate_b[MLSTM_HEADS:], (0, pad))
    return jnp.concatenate([w_i, w_f], axis=1).astype(BF16), jnp.concatenate([b_i, b_f]).reshape(1, 2 * LANES)


def _lane_spread_matrix():
    src = jnp.arange(LANES)[:, None]
    dst = jnp.arange(D_MODEL)[None, :] // LANES
    return jnp.tile((src == dst).astype(BF16), (3, 1))


def _mlstm_layer(x, w_in, w_gate, conv_w, gate_b, norm_w, w_out, layer, g, b,
                 w_gate_up, w_down, ffn_layer, g2, b2, tb):
    seq = x.shape[0]
    nb = seq // tb
    assert tb // CHUNK <= SUB
    wide = pltpu.VMEM((tb, D_MODEL), F32)
    sel = _lane_spread_matrix()
    spec0, spec1, spec2 = _pipeline_specs(tb, D_MODEL, nb, 3)
    vec = _const_spec((1, D_MODEL))
    return pl.pallas_call(
        functools.partial(_mlstm_kernel, tb=tb),
        out_shape=jax.ShapeDtypeStruct((seq, D_MODEL), F32),
        grid=(nb + 2,),
        in_specs=[spec0, spec1, _layer_spec(w_in.shape, layer, MLSTM_MAIN_W), _const_spec(w_gate.shape),
                  _const_spec(conv_w.shape), _const_spec((1, 2 * LANES)), vec,
                  _layer_spec(w_out.shape, layer), vec, vec, _const_spec((tb, tb)), _const_spec(sel.shape),
                  _layer_spec(w_gate_up.shape, ffn_layer), _layer_spec(w_down.shape, ffn_layer), vec, vec],
        out_specs=spec2,
        scratch_shapes=[pltpu.VMEM((tb + SUB, 2 * MLSTM_QK_W), F32),
                        pltpu.VMEM((tb, MLSTM_QK_W), F32), pltpu.VMEM((tb, MLSTM_QK_W), F32),
                        wide, pltpu.VMEM((tb // CHUNK, SUB, CHUNK), F32), wide, wide, wide, wide,
                        pltpu.VMEM((SUB, D_MODEL), F32), wide, wide,
                        pltpu.VMEM((MLSTM_HEADS // 2, 2 * MLSTM_DQK, 2 * MLSTM_DV), F32),
                        pltpu.VMEM((1, LANES), F32), wide],
        compiler_params=_params(),
        name="mlstm_layer",
    )(x, x, w_in, w_gate, conv_w, gate_b, norm_w, w_out, g, b, _chunk_tri(tb), sel, w_gate_up, w_down, g2, b2)


S5_LANE_GROUPS = LANES // S5_GROUP_CH
S5_BLOCKS = D_MODEL // LANES
S5_BLOCK_STATE = S5_LANE_GROUPS * S5_STATE
S5_TAPS = 4


def _s5_kernel(x_ref, win_ref, bre_ref, bim_ref, cre_ref, cim_ref, are_ref, aim_ref, d_ref, wout_ref,
               g_ref, b_ref, wgu_ref, wd_ref, g2_ref, b2_ref, o_ref, hr_s, hi_s, x1_s, *, tb):
    @pl.when(pl.program_id(0) == 0)
    def _():
        hr_s[...] = jnp.zeros_like(hr_s)
        hi_s[...] = jnp.zeros_like(hi_s)
        x1_s[...] = jnp.zeros_like(x1_s)

    ffn = _FfnPieces(x1_s[...], wgu_ref, wd_ref)
    x = x_ref[...]
    u = _dot(x.astype(BF16), win_ref[...])
    ub = u.astype(BF16)
    nt = tb // SUB
    row8 = lax.broadcasted_iota(jnp.int32, (SUB, S5_BLOCK_STATE), 0)

    def tile_scan(zr, zi, mults, first=0):
        for i in range(first, len(mults)):
            mr, mi = mults[i]
            rr = pltpu.roll(zr, 1 << i, 1)
            ri = pltpu.roll(zi, 1 << i, 1)
            zr, zi = zr + (mr * rr - mi * ri), zi + (mr * ri + mi * rr)
        return zr, zi

    row_u = lax.broadcasted_iota(jnp.int32, (tb, D_MODEL), 0) & (SUB - 1)
    u_taps = [ub] + [jnp.where(row_u >= k, pltpu.roll(u, k, 0), 0.0).astype(BF16) for k in range(1, S5_TAPS)]

    def input_states(j):
        lanes = slice(j * LANES, (j + 1) * LANES)
        uj = jnp.concatenate([t[:, lanes] for t in u_taps], axis=1)
        return (_dot(uj, bre_ref[j]).reshape(nt, SUB, S5_BLOCK_STATE),
                _dot(uj, bim_ref[j]).reshape(nt, SUB, S5_BLOCK_STATE))

    y_blocks = []
    for j in range(S5_BLOCKS):
        st = slice(j * S5_BLOCK_STATE, (j + 1) * S5_BLOCK_STATE)
        ar = are_ref[:, st]
        ai = aim_ref[:, st]
        mults = []
        pr, pi = ar, ai
        for i in range(3):
            keep = row8 >= (1 << i)
            mults.append((jnp.where(keep, pr, 0.0), jnp.where(keep, pi, 0.0)))
            pr, pi = pr * pr - pi * pi, 2.0 * pr * pi
        pwr, pwi = tile_scan(jnp.where(row8 == 0, ar, 0.0)[None], jnp.where(row8 == 0, ai, 0.0)[None], mults)
        pwr, pwi = pwr[0], pwi[0]
        sr, si = input_states(j)
        ffn.advance_to((j + 1) / S5_BLOCKS)
        sr, si = tile_scan(sr, si, mults, first=int(math.log2(S5_TAPS)))
        cr = hr_s[:, st]
        ci = hi_s[:, st]
        tiles_r, tiles_i = [], []
        for n in range(nt):
            tr = sr[n] + (pwr * cr - pwi * ci)
            ti = si[n] + (pwr * ci + pwi * cr)
            tiles_r.append(tr)
            tiles_i.append(ti)
            cr, ci = tr[SUB - 1:SUB, :], ti[SUB - 1:SUB, :]
        hr_s[:, st] = cr
        hi_s[:, st] = ci
        hr = jnp.concatenate(tiles_r, axis=0)
        hi = jnp.concatenate(tiles_i, axis=0)
        y_blocks.append(_dot(hr.astype(BF16), cre_ref[j]) - _dot(hi.astype(BF16), cim_ref[j]))
    o_ref[...] = ffn.finish(g2_ref, b2_ref)

    y = jax.nn.gelu(jnp.concatenate(y_blocks, axis=1) + d_ref[...] * u)
    z = _dot(y.astype(BF16), wout_ref[...])
    mix = z[:, :D_MODEL] * _sigmoid(z[:, D_MODEL:])
    x1_s[...] = _layernorm(ALPHA * x + mix, g_ref[...], b_ref[...])


def _s5_block_diag(t):
    a, b = t.shape[1], t.shape[2]
    t = t.reshape(S5_BLOCKS, S5_LANE_GROUPS, a, b)
    eye = jnp.eye(S5_LANE_GROUPS, dtype=t.dtype)
    out = jnp.einsum("jgab,gh->jgahb", t, eye)
    return out.reshape(S5_BLOCKS, S5_LANE_GROUPS * a, S5_LANE_GROUPS * b)


def _s5_layer(x, w_in, a_re, a_im, log_dt, b_re, b_im, c_re, c_im, d_skip, w_out, layer, g, b,
              w_gate_up, w_down, ffn_layer, g2, b2, tb):
    seq = x.shape[0]
    nb = seq // tb
    dt = jnp.exp(log_dt)[:, None]
    mag = jnp.exp(a_re * dt)
    abar_re = mag * jnp.cos(a_im * dt)
    abar_im = mag * jnp.sin(a_im * dt)
    nr = abar_re - 1.0
    ni = abar_im
    den = a_re * a_re + a_im * a_im
    coef_re = (nr * a_re + ni * a_im) / den
    coef_im = (ni * a_re - nr * a_im) / den
    bbar_re = coef_re[..., None] * b_re - coef_im[..., None] * b_im
    bbar_im = coef_re[..., None] * b_im + coef_im[..., None] * b_re
    taps_re, taps_im = [bbar_re], [bbar_im]
    for _ in range(1, S5_TAPS):
        pr, pi = taps_re[-1], taps_im[-1]
        taps_re.append(abar_re[..., None] * pr - abar_im[..., None] * pi)
        taps_im.append(abar_re[..., None] * pi + abar_im[..., None] * pr)
    stack_taps = lambda taps: jnp.concatenate(
        [_s5_block_diag(jnp.swapaxes(t, 1, 2)) for t in taps], axis=1).astype(BF16)
    bre = stack_taps(taps_re)
    bim = stack_taps(taps_im)
    cre = _s5_block_diag(jnp.swapaxes(c_re, 1, 2)).astype(BF16)
    cim = _s5_block_diag(jnp.swapaxes(c_im, 1, 2)).astype(BF16)
    n_state = S5_GROUPS * S5_STATE
    spec0, spec1 = _pipeline_specs(tb, D_MODEL, nb, 2)
    vec = _const_spec((1, D_MODEL))
    return pl.pallas_call(
        functools.partial(_s5_kernel, tb=tb),
        out_shape=jax.ShapeDtypeStruct((seq, D_MODEL), F32),
        grid=(nb + 1,),
        in_specs=[spec0, _layer_spec(w_in.shape, layer), _const_spec(bre.shape),
                  _const_spec(bim.shape), _const_spec(cre.shape), _const_spec(cim.shape), _const_spec((1, n_state)),
                  _const_spec((1, n_state)), vec, _layer_spec(w_out.shape, layer), vec, vec,
                  _layer_spec(w_gate_up.shape, ffn_layer), _layer_spec(w_down.shape, ffn_layer), vec, vec],
        out_specs=spec1,
        scratch_shapes=[pltpu.VMEM((1, n_state), F32), pltpu.VMEM((1, n_state), F32),
                        pltpu.VMEM((tb, D_MODEL), F32)],
        compiler_params=_params(),
        name="s5_layer",
    )(x, w_in, bre, bim, cre, cim, abar_re.reshape(1, n_state), abar_im.reshape(1, n_state),
      d_skip.reshape(1, D_MODEL), w_out, g, b, w_gate_up, w_down, g2, b2)


def _hgrn_lower_bounds(lb_logits):
    p = jax.nn.softmax(lb_logits.astype(F32), axis=0)
    c = jnp.cumsum(p, axis=0)
    return c - c[0:1]


def kernel(x, hgrn_w_in, hgrn_norm_w, hgrn_w_out, hgrn_lb_logits, mlstm_w_in, mlstm_conv_w, mlstm_gate_b, mlstm_norm_w, mlstm_w_out, s5_w_in, s5_a_re, s5_a_im, s5_log_dt, s5_b_re, s5_b_im, s5_c_re, s5_c_im, s5_d, s5_w_out, ffn_w_gate_up, ffn_w_down, ln_g, ln_b):
    bsz, seq, _ = x.shape
    tb = math.gcd(seq, ROW_BLOCK)
    assert tb % CHUNK == 0, "sequence length must be a multiple of the recurrence chunk"
    lb_all = _hgrn_lower_bounds(hgrn_lb_logits)
    row = lambda t: t.reshape(1, -1)
    (hgrn_w_in, hgrn_w_out, mlstm_w_in_b, mlstm_w_out, s5_w_in, s5_w_out, ffn_w_gate_up, ffn_w_down) = (
        w.astype(BF16) for w in (hgrn_w_in, hgrn_w_out, mlstm_w_in, mlstm_w_out, s5_w_in, s5_w_out,
                                 ffn_w_gate_up, ffn_w_down))
    outs = []
    for bi in range(bsz):
        h = x[bi]
        for i in range(DEPTH):
            kind = i % N_MIXERS
            j = i // N_MIXERS
            g0, b0 = row(ln_g[i, 0]), row(ln_b[i, 0])
            ffn_args = (ffn_w_gate_up, ffn_w_down, i, row(ln_g[i, 1]), row(ln_b[i, 1]), tb)
            if kind == 0:
                h = _hgrn_layer(h, hgrn_w_in, row(hgrn_norm_w[j]), hgrn_w_out, j, row(lb_all[i]), g0, b0, *ffn_args)
            elif kind == 1:
                w_gate, gate_b = _mlstm_gate_weights(mlstm_w_in[j], mlstm_gate_b[j])
                h = _mlstm_layer(h, mlstm_w_in_b, w_gate, mlstm_conv_w[j], gate_b, row(mlstm_norm_w[j]),
                                 mlstm_w_out, j, g0, b0, *ffn_args)
            else:
                h = _s5_layer(h, s5_w_in, s5_a_re[j], s5_a_im[j], s5_log_dt[j], s5_b_re[j], s5_b_im[j],
                              s5_c_re[j], s5_c_im[j], s5_d[j], s5_w_out, j, g0, b0, *ffn_args)
        outs.append(h)
    return jnp.stack(outs, axis=0)
```

```python
import functools
import math

import jax
import jax.numpy as jnp
from jax import lax
from jax.experimental import pallas as pl
from jax.experimental.pallas import tpu as pltpu

F32 = jnp.float32
BF16 = jnp.bfloat16

D_MODEL = 1024
DEPTH = 4
N_MIXERS = 3
ALPHA = (2.0 * DEPTH) ** 0.25
LN_EPS = 1e-5
HEAD_NORM_EPS = 1e-6

HGRN_HEADS = 8
HGRN_DK = 128
HGRN_DV = 128

MLSTM_HEADS = 8
MLSTM_DV = 128
MLSTM_DQK = 64
MLSTM_CONV = 4
MLSTM_QK_W = MLSTM_HEADS * MLSTM_DQK

S5_GROUP_CH = 16
S5_GROUPS = D_MODEL // S5_GROUP_CH
S5_STATE = 64

FFN_HIDDEN = -(-8 * D_MODEL // (3 * 256)) * 256

CHUNK = 128
SUB = 8
LANES = 128
ROW_BLOCK = 256
VMEM_LIMIT_BYTES = 60 * 1024 * 1024

NEG_INF = float("-inf")
BF16_MAX = float(jnp.finfo(jnp.bfloat16).max)


def _dot(a, b):
    return jnp.dot(a, b, preferred_element_type=F32)


def _dot_nt(a, b):
    return lax.dot_general(a, b, (((1,), (1,)), ((), ())), preferred_element_type=F32)


def _dot_tn(a, b):
    return lax.dot_general(a, b, (((0,), (0,)), ((), ())), preferred_element_type=F32)


def _layernorm(y, g, b):
    mu = jnp.mean(y, axis=-1, keepdims=True)
    yc = y - mu
    var = jnp.mean(yc * yc, axis=-1, keepdims=True)
    return yc * lax.rsqrt(var + LN_EPS) * g + b


LOG2E = 1.4426950408889634


def _log1p_exp_neg(z):
    return jnp.log(1.0 + jnp.exp2(z * -LOG2E))


def _log_sigmoid(z):
    return jnp.minimum(z, 0.0) - _log1p_exp_neg(jnp.abs(z))


def _sigmoid(z):
    return 1.0 / (1.0 + jnp.exp2(z * -LOG2E))


def _chunk_cumsum(tri, z):
    hi = z.astype(BF16)
    r1 = z - hi.astype(F32)
    mid = r1.astype(BF16)
    lo = (r1 - mid.astype(F32)).astype(BF16)
    return _dot(tri, hi) + _dot(tri, mid) + _dot(tri, lo)


def _const_spec(shape):
    nd = len(shape)
    return pl.BlockSpec(shape, lambda i: (0,) * nd, pipeline_mode=pl.Buffered(1))


def _layer_spec(stacked_shape, layer, cols=None):
    _, rows, width = stacked_shape
    return pl.BlockSpec((None, rows, cols or width), lambda i: (layer, 0, 0), pipeline_mode=pl.Buffered(1))


def _pipeline_specs(tb, width, nb, stages):
    return [pl.BlockSpec((tb, width), functools.partial(lambda i, s: (jnp.clip(i - s, 0, nb - 1), 0), s=s))
            for s in range(stages)]


def _params():
    return pltpu.CompilerParams(dimension_semantics=("arbitrary",), vmem_limit_bytes=VMEM_LIMIT_BYTES)


FFN_PIECE = 256


class _FfnPieces:
    def __init__(self, x, wgu_ref, wd_ref):
        self.x = x
        self.xb = x.astype(BF16)
        self.wgu_ref = wgu_ref
        self.wd_ref = wd_ref
        self.n_up = FFN_HIDDEN // FFN_PIECE
        self.n_down = D_MODEL // FFN_PIECE
        self.done = 0
        self.act = []
        self.out = []

    def advance_to(self, fraction):
        total = self.n_up + self.n_down
        target = min(total, int(fraction * total + 1e-9))
        while self.done < target:
            if self.done < self.n_up:
                lo = self.done * FFN_PIECE
                gate = _dot(self.xb, self.wgu_ref[:, lo:lo + FFN_PIECE])
                up = _dot(self.xb, self.wgu_ref[:, FFN_HIDDEN + lo:FFN_HIDDEN + lo + FFN_PIECE])
                self.act.append((gate * _sigmoid(gate) * up).astype(BF16))
            else:
                if len(self.act) > 1:
                    self.act = [jnp.concatenate(self.act, axis=1)]
                lo = (self.done - self.n_up) * FFN_PIECE
                self.out.append(_dot(self.act[0], self.wd_ref[:, lo:lo + FFN_PIECE]))
            self.done += 1

    def finish(self, g_ref, b_ref):
        self.advance_to(1.0)
        return _layernorm(ALPHA * self.x + jnp.concatenate(self.out, axis=1), g_ref[...], b_ref[...])


def _hgrn_kernel(xf_ref, xb_ref, win_ref, lb_ref, nw_ref, wout_ref, g_ref, b_ref, tri_ref,
                 wgu_ref, wd_ref, g2_ref, b2_ref, o_ref,
                 q_s, k_s, v_s, b_s, gate_s, o_s, state_s, x1_s, *, tb):
    nc = tb // CHUNK

    @pl.when(pl.program_id(0) == 0)
    def _():
        state_s[...] = jnp.zeros_like(state_s)
        for ref in (q_s, k_s, v_s, b_s, gate_s, x1_s):
            ref[...] = jnp.zeros_like(ref)

    ffn = _FfnPieces(x1_s[...], wgu_ref, wd_ref)

    proj = _dot(xf_ref[...].astype(BF16), win_ref[...])
    q = proj[:, 0:D_MODEL]
    f = proj[:, D_MODEL:2 * D_MODEL]
    v = proj[:, 2 * D_MODEL:3 * D_MODEL]
    gt = proj[:, 3 * D_MODEL:4 * D_MODEL]
    lb = lb_ref[...]
    a1 = jnp.log(lb)
    a2 = jnp.log1p(-lb) + _log_sigmoid(f)
    log_f = jnp.maximum(a1, a2) + _log1p_exp_neg(jnp.abs(a1 - a2))
    k = (1.0 - lb) * _sigmoid(-f)
    q = q * _sigmoid(q)
    gate = gt * _sigmoid(gt)
    bcum = _chunk_cumsum(tri_ref[...], log_f) * LOG2E

    row = lax.broadcasted_iota(jnp.int32, (CHUNK, LANES), 0)
    rowi = lax.broadcasted_iota(jnp.int32, (CHUNK, CHUNK), 0)
    coli = lax.broadcasted_iota(jnp.int32, (CHUNK, CHUNK), 1)
    row8 = lax.broadcasted_iota(jnp.int32, (SUB, CHUNK), 0)
    lane8 = lax.broadcasted_iota(jnp.int32, (SUB, CHUNK), 1)
    place = [[((lane8 == blk * SUB + s) & (row8 >= s)).astype(F32) for s in range(SUB)]
             for blk in range(CHUNK // SUB)]
    same_block = {}
    m = SUB
    while m < CHUNK:
        shift = int(math.log2(2 * m))
        same_block[m] = ((rowi >> shift) == (coli >> shift)).astype(F32)
        m *= 2
    nw = nw_ref[...]

    def head_scores(h, rows):
        qc = q_s[h, rows, :]
        kc = k_s[h, rows, :]
        bc = b_s[h, rows, :]
        att = jnp.zeros((CHUNK, CHUNK), F32)
        m = SUB
        while m < CHUNK:
            ref = jnp.concatenate(
                [jnp.broadcast_to(bc[blk * 2 * m + m - 1:blk * 2 * m + m, :], (2 * m, LANES))
                 for blk in range(CHUNK // (2 * m))], axis=0)
            upper = (row & m) != 0
            qh = qc * jnp.exp2(jnp.where(upper, bc - ref, NEG_INF))
            kh = kc * jnp.exp2(jnp.where(upper, NEG_INF, ref - bc))
            pair = _dot_nt(qh.astype(BF16), kh.astype(BF16))
            att = att + pair * same_block[m]
            m *= 2
        tiles = []
        for blk in range(CHUNK // SUB):
            sl = slice(blk * SUB, (blk + 1) * SUB)
            qi, bi = qc[sl], bc[sl]
            tile = jnp.zeros((SUB, CHUNK), F32)
            for s in range(SUB):
                r = rows.start + blk * SUB + s
                ks = jnp.broadcast_to(k_s[h, r:r + 1, :], (SUB, LANES))
                bs = jnp.broadcast_to(b_s[h, r:r + 1, :], (SUB, LANES))
                decay = jnp.exp2(jnp.minimum(bi - bs, 0.0))
                col = jnp.sum(qi * (ks * decay), axis=1, keepdims=True)
                tile = tile + col * place[blk][s]
            tiles.append(tile)
        return (att + jnp.concatenate(tiles, axis=0)).astype(BF16)

    def head_output(h, rows, att):
        st = state_s[h]
        qc = q_s[h, rows, :]
        kc = k_s[h, rows, :]
        vcb = v_s[h, rows, :].astype(BF16)
        bc = b_s[h, rows, :]
        b_last = bc[CHUNK - 1:CHUNK, :]
        o_s[h, rows, :] = _dot(att, vcb) + _dot_nt((qc * jnp.exp2(bc)).astype(BF16), st.astype(BF16))
        k_dec = kc * jnp.exp2(b_last - bc)
        state_s[h] = st * jnp.exp2(b_last) + _dot_tn(vcb, k_dec.astype(BF16))

    n_units = 2 * nc * HGRN_HEADS
    unit = 0
    for c in range(nc):
        rows = slice(c * CHUNK, (c + 1) * CHUNK)
        atts = []
        for h in range(HGRN_HEADS):
            atts.append(head_scores(h, rows))
            unit += 1
            ffn.advance_to(unit / n_units)
        for h in range(HGRN_HEADS):
            head_output(h, rows, atts[h])
            unit += 1
            ffn.advance_to(unit / n_units)
    o_ref[...] = ffn.finish(g2_ref, b2_ref)

    normed = []
    for h in range(HGRN_HEADS):
        o = o_s[h]
        ms = jnp.mean(o * o, axis=-1, keepdims=True)
        normed.append((o * lax.rsqrt(ms + HEAD_NORM_EPS) * nw * gate_s[h]).astype(BF16))
    y = _dot(jnp.concatenate(normed, axis=1), wout_ref[...])
    x1_new = _layernorm(ALPHA * xb_ref[...] + y, g_ref[...], b_ref[...])

    x1_s[...] = x1_new
    for h in range(HGRN_HEADS):
        sl = slice(h * HGRN_DK, (h + 1) * HGRN_DK)
        q_s[h] = q[:, sl]
        k_s[h] = k[:, sl]
        v_s[h] = v[:, sl]
        b_s[h] = bcum[:, sl]
        gate_s[h] = gate[:, sl]


def _chunk_tri(tb):
    i = jnp.arange(tb)
    same = (i[:, None] // CHUNK) == (i[None, :] // CHUNK)
    return (same & (i[None, :] <= i[:, None])).astype(BF16)


def _hgrn_layer(x, w_in, norm_w, w_out, layer, lb, g, b, w_gate_up, w_down, ffn_layer, g2, b2, tb):
    seq = x.shape[0]
    nb = seq // tb
    head_scratch = pltpu.VMEM((HGRN_HEADS, tb, HGRN_DK), F32)
    spec0, spec1, spec2 = _pipeline_specs(tb, D_MODEL, nb, 3)
    vec = _const_spec((1, D_MODEL))
    return pl.pallas_call(
        functools.partial(_hgrn_kernel, tb=tb),
        out_shape=jax.ShapeDtypeStruct((seq, D_MODEL), F32),
        grid=(nb + 2,),
        in_specs=[spec0, spec1, _layer_spec(w_in.shape, layer), vec, _const_spec((1, HGRN_DV)),
                  _layer_spec(w_out.shape, layer), vec, vec, _const_spec((tb, tb)),
                  _layer_spec(w_gate_up.shape, ffn_layer), _layer_spec(w_down.shape, ffn_layer), vec, vec],
        out_specs=spec2,
        scratch_shapes=[head_scratch] * 6 + [pltpu.VMEM((HGRN_HEADS, HGRN_DV, HGRN_DK), F32),
                                             pltpu.VMEM((tb, D_MODEL), F32)],
        compiler_params=_params(),
        name="hgrn_layer",
    )(x, x, w_in, lb, norm_w, w_out, g, b, _chunk_tri(tb), w_gate_up, w_down, g2, b2)


def _mlstm_kernel(xf_ref, xb_ref, win_ref, wg_ref, conv_ref, gb_ref, nw_ref, wout_ref, g_ref, b_ref, tri_ref, sel_ref,
                  wgu_ref, wd_ref, g2_ref, b2_ref, o_ref,
                  pad_s, q_s, k_s, v_s, crow_s, gcol_s, wi_s, em_s, ws_s, wp_s, sigo_s, h_s, c_s, m_s, x1_s, *, tb):
    nc = tb // CHUNK
    halo = SUB

    @pl.when(pl.program_id(0) == 0)
    def _():
        pad_s[0:halo, :] = jnp.zeros((halo, 2 * MLSTM_QK_W), F32)
        c_s[...] = jnp.zeros_like(c_s)
        m_s[...] = jnp.zeros_like(m_s)
        for ref in (q_s, k_s, v_s, crow_s, gcol_s, wi_s, ws_s, wp_s, sigo_s, x1_s):
            ref[...] = jnp.zeros_like(ref)
        em_s[...] = jnp.ones_like(em_s)

    ffn = _FfnPieces(x1_s[...], wgu_ref, wd_ref)
    xb = xf_ref[...].astype(BF16)
    proj = _dot(xb, win_ref[...])
    v_new = proj[:, 2 * MLSTM_QK_W:2 * MLSTM_QK_W + D_MODEL]
    o_pre = proj[:, 2 * MLSTM_QK_W + D_MODEL:]
    pad_s[halo:halo + tb, :] = proj[:, :2 * MLSTM_QK_W]
    conv_w = conv_ref[...]
    qk = pad_s[pl.ds(halo - MLSTM_CONV + 1, tb), :] * conv_w[0:1, :]
    for j in range(1, MLSTM_CONV):
        qk = qk + pad_s[pl.ds(halo - MLSTM_CONV + 1 + j, tb), :] * conv_w[j:j + 1, :]
    pad_s[0:halo, :] = pad_s[tb:tb + halo, :]
    qk = qk * _sigmoid(qk)
    q_new = qk[:, :MLSTM_QK_W]
    k_new = qk[:, MLSTM_QK_W:] * (MLSTM_DQK ** -0.5)
    gates = _dot(xb, wg_ref[...]) + gb_ref[...]
    li = gates[:, :LANES]
    bcum = _chunk_cumsum(tri_ref[...], _log_sigmoid(gates[:, LANES:]))
    cq = li - bcum
    rin = lax.broadcasted_iota(jnp.int32, (tb, LANES), 0) & (CHUNK - 1)
    cm = cq
    k = 1
    while k < CHUNK:
        shifted = pltpu.roll(cm, k, 0)
        cm = jnp.maximum(cm, jnp.where(rin >= k, shifted, NEG_INF))
        k *= 2
    m_prev = m_s[...]
    g_parts, wi_parts, em_parts, ws_parts, wp_parts, crow_new = [], [], [], [], [], []
    for c in range(nc):
        sl = slice(c * CHUNK, (c + 1) * CHUNK)
        mx = jnp.maximum(m_prev, cm[(c + 1) * CHUNK - 1:(c + 1) * CHUNK])
        g_c = jnp.maximum(cm[sl], m_prev)
        g_parts.append(g_c)
        wi_parts.append(jnp.exp(m_prev - g_c))
        em_parts.append(jnp.minimum(jnp.exp(-(bcum[sl] + g_c)), BF16_MAX))
        ws_parts.append(jnp.exp(cq[sl] - mx))
        wp_parts.append(jnp.exp(m_prev - mx))
        m_prev = bcum[(c + 1) * CHUNK - 1:(c + 1) * CHUNK] + mx
        crow_new.append(cq[sl].T[0:SUB, :])
    m_s[...] = m_prev
    wp_parts.append(jnp.zeros((SUB - nc, LANES), F32))
    def spread_lanes(tile, terms):
        parts, rest = [], tile
        for _ in range(terms):
            part = rest.astype(BF16)
            parts.append(part)
            rest = rest - part.astype(F32)
        return _dot(jnp.concatenate(parts, axis=1), sel_ref[0:terms * LANES, :])

    g_spread = spread_lanes(jnp.concatenate(g_parts, axis=0), 3)
    w_spread = spread_lanes(jnp.concatenate(wi_parts + em_parts + ws_parts + wp_parts, axis=0), 2)
    sig_o = _sigmoid(o_pre)

    lane_c = lax.broadcasted_iota(jnp.int32, (CHUNK, LANES), 1)
    rowi = lax.broadcasted_iota(jnp.int32, (CHUNK, CHUNK), 0)
    coli = lax.broadcasted_iota(jnp.int32, (CHUNK, CHUNK), 1)
    causal = rowi >= coli
    first_head = lane_c < MLSTM_DQK
    first_rows = lax.broadcasted_iota(jnp.int32, (2 * MLSTM_DQK, 2 * MLSTM_DV), 0) < MLSTM_DQK
    ones_blk = jnp.ones((CHUNK, LANES), BF16)
    nw_all = nw_ref[...]

    n_pairs = MLSTM_HEADS // 2
    n_units = 2 * nc * n_pairs
    unit = 0
    for c in range(nc):
        rows = slice(c * CHUNK, (c + 1) * CHUNK)
        crow = crow_s[c]
        wp_row = wp_s[c:c + 1, :]
        staged = []
        for p in range(n_pairs):
            lanes_p = slice(p * LANES, (p + 1) * LANES)
            cext = c_s[p]
            q_pair = q_s[rows, lanes_p]
            k_pair = k_s[rows, lanes_p]
            hsl = [slice((2 * p + j) * MLSTM_DV, (2 * p + j + 1) * MLSTM_DV) for j in range(2)]
            qm = jnp.concatenate([jnp.where(first_head, q_pair, 0.0), jnp.where(first_head, 0.0, q_pair)],
                                 axis=0).astype(BF16)
            v_ext = [jnp.concatenate([v_s[rows, hsl[j]].astype(BF16), ones_blk], axis=1) for j in range(2)]
            scores = _dot_nt(qm, k_pair.astype(BF16))
            inter = _dot(qm, cext.astype(BF16))
            kw = jnp.concatenate([jnp.where(first_head, k_pair * ws_s[rows, hsl[0]], 0.0),
                                  jnp.where(first_head, 0.0, k_pair * ws_s[rows, hsl[1]])], axis=0).astype(BF16)
            update = _dot_tn(kw, jnp.concatenate(v_ext, axis=0))
            w_prev = jnp.where(first_rows, jnp.tile(wp_row[:, hsl[0]], (1, 2)), jnp.tile(wp_row[:, hsl[1]], (1, 2)))
            c_s[p] = w_prev * cext + update
            staged.append((hsl, v_ext, scores, inter))
            unit += 1
            ffn.advance_to(unit / n_units)
        for p in range(n_pairs):
            hsl, v_ext, scores, inter = staged[p]
            for j in range(2):
                h = 2 * p + j
                hr = slice(j * CHUNK, (j + 1) * CHUNK)
                d_mat = jnp.exp(jnp.where(causal, crow[h:h + 1, :] - gcol_s[rows, h * MLSTM_DV:h * MLSTM_DV + CHUNK],
                                          NEG_INF))
                s = scores[hr] * d_mat
                both = _dot(s.astype(BF16), v_ext[j]) + jnp.tile(wi_s[rows, hsl[j]], (1, 2)) * inter[hr]
                floor = em_s[rows, hsl[j]]
                hh = both[:, :MLSTM_DV] / jnp.maximum(jnp.abs(both[:, MLSTM_DV:]), floor)
                h_s[rows, hsl[j]] = jnp.where(floor >= BF16_MAX, 0.0, hh)
            unit += 1
            ffn.advance_to(unit / n_units)
    o_ref[...] = ffn.finish(g2_ref, b2_ref)

    normed = []
    for h in range(MLSTM_HEADS):
        hsl = slice(h * MLSTM_DV, (h + 1) * MLSTM_DV)
        hh = h_s[:, hsl]
        ms = jnp.mean(hh * hh, axis=-1, keepdims=True)
        normed.append((hh * lax.rsqrt(ms + HEAD_NORM_EPS) * nw_all[:, hsl] * sigo_s[:, hsl]).astype(BF16))
    y = _dot(jnp.concatenate(normed, axis=1), wout_ref[...])
    x1_new = _layernorm(ALPHA * xb_ref[...] + y, g_ref[...], b_ref[...])

    x1_s[...] = x1_new
    q_s[...] = q_new
    k_s[...] = k_new
    v_s[...] = v_new
    sigo_s[...] = sig_o
    for c in range(nc):
        crow_s[c] = crow_new[c]
    gcol_s[...] = g_spread
    wi_s[...] = w_spread[0:tb]
    em_s[...] = w_spread[tb:2 * tb]
    ws_s[...] = w_spread[2 * tb:3 * tb]
    wp_s[...] = w_spread[3 * tb:3 * tb + SUB]


MLSTM_MAIN_W = 2 * MLSTM_QK_W + 2 * D_MODEL


def _mlstm_gate_weights(w_in, gate_b):
    pad = LANES - MLSTM_HEADS
    w_i = jnp.pad(w_in[:, MLSTM_MAIN_W:MLSTM_MAIN_W + MLSTM_HEADS], ((0, 0), (0, pad)))
    w_f = jnp.pad(w_in[:, MLSTM_MAIN_W + MLSTM_HEADS:], ((0, 0), (0, pad)))
    b_i = jnp.pad(gate_b[:MLSTM_HEADS], (0, pad))
    b_f = jnp.pad(gate_b[MLSTM_HEADS:], (0, pad))
    return jnp.concatenate([w_i, w_f], axis=1).astype(BF16), jnp.concatenate([b_i, b_f]).reshape(1, 2 * LANES)


def _lane_spread_matrix():
    src = jnp.arange(LANES)[:, None]
    dst = jnp.arange(D_MODEL)[None, :] // LANES
    return jnp.tile((src == dst).astype(BF16), (3, 1))


def _mlstm_layer(x, w_in, w_gate, conv_w, gate_b, norm_w, w_out, layer, g, b,
                 w_gate_up, w_down, ffn_layer, g2, b2, tb):
    seq = x.shape[0]
    nb = seq // tb
    assert tb // CHUNK <= SUB
    wide = pltpu.VMEM((tb, D_MODEL), F32)
    sel = _lane_spread_matrix()
    spec0, spec1, spec2 = _pipeline_specs(tb, D_MODEL, nb, 3)
    vec = _const_spec((1, D_MODEL))
    return pl.pallas_call(
        functools.partial(_mlstm_kernel, tb=tb),
        out_shape=jax.ShapeDtypeStruct((seq, D_MODEL), F32),
        grid=(nb + 2,),
        in_specs=[spec0, spec1, _layer_spec(w_in.shape, layer, MLSTM_MAIN_W), _const_spec(w_gate.shape),
                  _const_spec(conv_w.shape), _const_spec((1, 2 * LANES)), vec,
                  _layer_spec(w_out.shape, layer), vec, vec, _const_spec((tb, tb)), _const_spec(sel.shape),
                  _layer_spec(w_gate_up.shape, ffn_layer), _layer_spec(w_down.shape, ffn_layer), vec, vec],
        out_specs=spec2,
        scratch_shapes=[pltpu.VMEM((tb + SUB, 2 * MLSTM_QK_W), F32),
                        pltpu.VMEM((tb, MLSTM_QK_W), F32), pltpu.VMEM((tb, MLSTM_QK_W), F32),
                        wide, pltpu.VMEM((tb // CHUNK, SUB, CHUNK), F32), wide, wide, wide, wide,
                        pltpu.VMEM((SUB, D_MODEL), F32), wide, wide,
                        pltpu.VMEM((MLSTM_HEADS // 2, 2 * MLSTM_DQK, 2 * MLSTM_DV), F32),
                        pltpu.VMEM((1, LANES), F32), wide],
        compiler_params=_params(),
        name="mlstm_layer",
    )(x, x, w_in, w_gate, conv_w, gate_b, norm_w, w_out, g, b, _chunk_tri(tb), sel, w_gate_up, w_down, g2, b2)


S5_LANE_GROUPS = LANES // S5_GROUP_CH
S5_BLOCKS = D_MODEL // LANES
S5_BLOCK_STATE = S5_LANE_GROUPS * S5_STATE
S5_SCAN_WIDTH = 256
S5_TAPS = 4


def _s5_kernel(x_ref, win_ref, bre_ref, bim_ref, cre_ref, cim_ref, are_ref, aim_ref, d_ref, wout_ref,
               g_ref, b_ref, wgu_ref, wd_ref, g2_ref, b2_ref, o_ref, hr_s, hi_s, x1_s, *, tb):
    @pl.when(pl.program_id(0) == 0)
    def _():
        hr_s[...] = jnp.zeros_like(hr_s)
        hi_s[...] = jnp.zeros_like(hi_s)
        x1_s[...] = jnp.zeros_like(x1_s)

    ffn = _FfnPieces(x1_s[...], wgu_ref, wd_ref)
    x = x_ref[...]
    u = _dot(x.astype(BF16), win_ref[...])
    ub = u.astype(BF16)
    nt = tb // SUB
    row8 = lax.broadcasted_iota(jnp.int32, (SUB, S5_SCAN_WIDTH), 0)

    def tile_scan(zr, zi, mults, first=0):
        for i in range(first, len(mults)):
            mr, mi = mults[i]
            rr = pltpu.roll(zr, 1 << i, 1)
            ri = pltpu.roll(zi, 1 << i, 1)
            zr, zi = zr + (mr * rr - mi * ri), zi + (mr * ri + mi * rr)
        return zr, zi

    row_u = lax.broadcasted_iota(jnp.int32, (tb, D_MODEL), 0) & (SUB - 1)
    u_taps = [ub] + [jnp.where(row_u >= k, pltpu.roll(u, k, 0), 0.0).astype(BF16) for k in range(1, S5_TAPS)]

    def input_states(j, cols):
        lanes = slice(j * LANES, (j + 1) * LANES)
        uj = jnp.concatenate([t[:, lanes] for t in u_taps], axis=1)
        return (_dot(uj, bre_ref[j, :, cols]).reshape(nt, SUB, S5_SCAN_WIDTH),
                _dot(uj, bim_ref[j, :, cols]).reshape(nt, SUB, S5_SCAN_WIDTH))

    n_slabs = S5_BLOCK_STATE // S5_SCAN_WIDTH
    y_blocks = []
    for j, slab in [(j, slab) for j in range(S5_BLOCKS) for slab in range(n_slabs)]:
        cols = slice(slab * S5_SCAN_WIDTH, (slab + 1) * S5_SCAN_WIDTH)
        st = slice(j * S5_BLOCK_STATE + cols.start, j * S5_BLOCK_STATE + cols.stop)
        ar = are_ref[:, st]
        ai = aim_ref[:, st]
        mults = []
        pr, pi = ar, ai
        for i in range(3):
            keep = row8 >= (1 << i)
            mults.append((jnp.where(keep, pr, 0.0), jnp.where(keep, pi, 0.0)))
            pr, pi = pr * pr - pi * pi, 2.0 * pr * pi
        pwr, pwi = tile_scan(jnp.where(row8 == 0, ar, 0.0)[None], jnp.where(row8 == 0, ai, 0.0)[None], mults)
        pwr, pwi = pwr[0], pwi[0]
        sr, si = input_states(j, cols)
        ffn.advance_to((j * n_slabs + slab + 1) / (S5_BLOCKS * n_slabs))
        sr, si = tile_scan(sr, si, mults, first=int(math.log2(S5_TAPS)))
        cr = hr_s[:, st]
        ci = hi_s[:, st]
        tiles_r, tiles_i = [], []
        for n in range(nt):
            tr = sr[n] + (pwr * cr - pwi * ci)
            ti = si[n] + (pwr * ci + pwi * cr)
            tiles_r.append(tr)
            tiles_i.append(ti)
            cr, ci = tr[SUB - 1:SUB, :], ti[SUB - 1:SUB, :]
        hr_s[:, st] = cr
        hi_s[:, st] = ci
        hr = jnp.concatenate(tiles_r, axis=0)
        hi = jnp.concatenate(tiles_i, axis=0)
        y_slab = _dot(hr.astype(BF16), cre_ref[j, cols, :]) - _dot(hi.astype(BF16), cim_ref[j, cols, :])
        if slab == 0:
            y_blocks.append(y_slab)
        else:
            y_blocks[j] = y_blocks[j] + y_slab
    o_ref[...] = ffn.finish(g2_ref, b2_ref)

    y = jax.nn.gelu(jnp.concatenate(y_blocks, axis=1) + d_ref[...] * u)
    z = _dot(y.astype(BF16), wout_ref[...])
    mix = z[:, :D_MODEL] * _sigmoid(z[:, D_MODEL:])
    x1_s[...] = _layernorm(ALPHA * x + mix, g_ref[...], b_ref[...])


def _s5_block_diag(t):
    a, b = t.shape[1], t.shape[2]
    t = t.reshape(S5_BLOCKS, S5_LANE_GROUPS, a, b)
    eye = jnp.eye(S5_LANE_GROUPS, dtype=t.dtype)
    out = jnp.einsum("jgab,gh->jgahb", t, eye)
    return out.reshape(S5_BLOCKS, S5_LANE_GROUPS * a, S5_LANE_GROUPS * b)


def _s5_layer(x, w_in, a_re, a_im, log_dt, b_re, b_im, c_re, c_im, d_skip, w_out, layer, g, b,
              w_gate_up, w_down, ffn_layer, g2, b2, tb):
    seq = x.shape[0]
    nb = seq // tb
    dt = jnp.exp(log_dt)[:, None]
    mag = jnp.exp(a_re * dt)
    abar_re = mag * jnp.cos(a_im * dt)
    abar_im = mag * jnp.sin(a_im * dt)
    nr = abar_re - 1.0
    ni = abar_im
    den = a_re * a_re + a_im * a_im
    coef_re = (nr * a_re + ni * a_im) / den
    coef_im = (ni * a_re - nr * a_im) / den
    bbar_re = coef_re[..., None] * b_re - coef_im[..., None] * b_im
    bbar_im = coef_re[..., None] * b_im + coef_im[..., None] * b_re
    taps_re, taps_im = [bbar_re], [bbar_im]
    for _ in range(1, S5_TAPS):
        pr, pi = taps_re[-1], taps_im[-1]
        taps_re.append(abar_re[..., None] * pr - abar_im[..., None] * pi)
        taps_im.append(abar_re[..., None] * pi + abar_im[..., None] * pr)
    stack_taps = lambda taps: jnp.concatenate(
        [_s5_block_diag(jnp.swapaxes(t, 1, 2)) for t in taps], axis=1).astype(BF16)
    bre = stack_taps(taps_re)
    bim = stack_taps(taps_im)
    cre = _s5_block_diag(jnp.swapaxes(c_re, 1, 2)).astype(BF16)
    cim = _s5_block_diag(jnp.swapaxes(c_im, 1, 2)).astype(BF16)
    n_state = S5_GROUPS * S5_STATE
    spec0, spec1 = _pipeline_specs(tb, D_MODEL, nb, 2)
    vec = _const_spec((1, D_MODEL))
    return pl.pallas_call(
        functools.partial(_s5_kernel, tb=tb),
        out_shape=jax.ShapeDtypeStruct((seq, D_MODEL), F32),
        grid=(nb + 1,),
        in_specs=[spec0, _layer_spec(w_in.shape, layer), _const_spec(bre.shape),
                  _const_spec(bim.shape), _const_spec(cre.shape), _const_spec(cim.shape), _const_spec((1, n_state)),
                  _const_spec((1, n_state)), vec, _layer_spec(w_out.shape, layer), vec, vec,
                  _layer_spec(w_gate_up.shape, ffn_layer), _layer_spec(w_down.shape, ffn_layer), vec, vec],
        out_specs=spec1,
        scratch_shapes=[pltpu.VMEM((1, n_state), F32), pltpu.VMEM((1, n_state), F32),
                        pltpu.VMEM((tb, D_MODEL), F32)],
        compiler_params=_params(),
        name="s5_layer",
    )(x, w_in, bre, bim, cre, cim, abar_re.reshape(1, n_state), abar_im.reshape(1, n_state),
      d_skip.reshape(1, D_MODEL), w_out, g, b, w_gate_up, w_down, g2, b2)


def _hgrn_lower_bounds(lb_logits):
    p = jax.nn.softmax(lb_logits.astype(F32), axis=0)
    c = jnp.cumsum(p, axis=0)
    return c - c[0:1]


def kernel(x, hgrn_w_in, hgrn_norm_w, hgrn_w_out, hgrn_lb_logits, mlstm_w_in, mlstm_conv_w, mlstm_gate_b, mlstm_norm_w, mlstm_w_out, s5_w_in, s5_a_re, s5_a_im, s5_log_dt, s5_b_re, s5_b_im, s5_c_re, s5_c_im, s5_d, s5_w_out, ffn_w_gate_up, ffn_w_down, ln_g, ln_b):
    bsz, seq, _ = x.shape
    tb = math.gcd(seq, ROW_BLOCK)
    assert tb % CHUNK == 0, "sequence length must be a multiple of the recurrence chunk"
    lb_all = _hgrn_lower_bounds(hgrn_lb_logits)
    row = lambda t: t.reshape(1, -1)
    (hgrn_w_in, hgrn_w_out, mlstm_w_in_b, mlstm_w_out, s5_w_in, s5_w_out, ffn_w_gate_up, ffn_w_down) = (
        w.astype(BF16) for w in (hgrn_w_in, hgrn_w_out, mlstm_w_in, mlstm_w_out, s5_w_in, s5_w_out,
                                 ffn_w_gate_up, ffn_w_down))
    outs = []
    for bi in range(bsz):
        h = x[bi]
        for i in range(DEPTH):
            kind = i % N_MIXERS
            j = i // N_MIXERS
            g0, b0 = row(ln_g[i, 0]), row(ln_b[i, 0])
            ffn_args = (ffn_w_gate_up, ffn_w_down, i, row(ln_g[i, 1]), row(ln_b[i, 1]), tb)
            if kind == 0:
                h = _hgrn_layer(h, hgrn_w_in, row(hgrn_norm_w[j]), hgrn_w_out, j, row(lb_all[i]), g0, b0, *ffn_args)
            elif kind == 1:
                w_gate, gate_b = _mlstm_gate_weights(mlstm_w_in[j], mlstm_gate_b[j])
                h = _mlstm_layer(h, mlstm_w_in_b, w_gate, mlstm_conv_w[j], gate_b, row(mlstm_norm_w[j]),
                                 mlstm_w_out, j, g0, b0, *ffn_args)
            else:
                h = _s5_layer(h, s5_w_in, s5_a_re[j], s5_a_im[j], s5_log_dt[j], s5_b_re[j], s5_b_im[j],
                              s5_c_re[j], s5_c_im[j], s5_d[j], s5_w_out, j, g0, b0, *ffn_args)
        outs.append(h)
    return jnp.stack(outs, axis=0)
```

```python
import functools
import math

import jax
import jax.numpy as jnp
from jax import lax
from jax.experimental import pallas as pl
from jax.experimental.pallas import tpu as pltpu

F32 = jnp.float32
BF16 = jnp.bfloat16

D_MODEL = 1024
DEPTH = 4
N_MIXERS = 3
ALPHA = (2.0 * DEPTH) ** 0.25
LN_EPS = 1e-5
HEAD_NORM_EPS = 1e-6

HGRN_HEADS = 8
HGRN_DK = 128
HGRN_DV = 128

MLSTM_HEADS = 8
MLSTM_DV = 128
MLSTM_DQK = 64
MLSTM_CONV = 4
MLSTM_QK_W = MLSTM_HEADS * MLSTM_DQK

S5_GROUP_CH = 16
S5_GROUPS = D_MODEL // S5_GROUP_CH
S5_STATE = 64

FFN_HIDDEN = -(-8 * D_MODEL // (3 * 256)) * 256

CHUNK = 128
SUB = 8
LANES = 128
ROW_BLOCK = 256
VMEM_LIMIT_BYTES = 60 * 1024 * 1024

NEG_INF = float("-inf")
BF16_MAX = float(jnp.finfo(jnp.bfloat16).max)


def _dot(a, b):
    return jnp.dot(a, b, preferred_element_type=F32)


def _dot_nt(a, b):
    return lax.dot_general(a, b, (((1,), (1,)), ((), ())), preferred_element_type=F32)


def _dot_tn(a, b):
    return lax.dot_general(a, b, (((0,), (0,)), ((), ())), preferred_element_type=F32)


def _layernorm(y, g, b):
    mu = jnp.mean(y, axis=-1, keepdims=True)
    yc = y - mu
    var = jnp.mean(yc * yc, axis=-1, keepdims=True)
    return yc * lax.rsqrt(var + LN_EPS) * g + b


LOG2E = 1.4426950408889634


def _log1p_exp_neg(z):
    return jnp.log(1.0 + jnp.exp2(z * -LOG2E))


def _log_sigmoid(z):
    return jnp.minimum(z, 0.0) - _log1p_exp_neg(jnp.abs(z))


def _sigmoid(z):
    return 1.0 / (1.0 + jnp.exp2(z * -LOG2E))


def _chunk_cumsum(tri, z):
    hi = z.astype(BF16)
    r1 = z - hi.astype(F32)
    mid = r1.astype(BF16)
    lo = (r1 - mid.astype(F32)).astype(BF16)
    return _dot(tri, hi) + _dot(tri, mid) + _dot(tri, lo)


def _const_spec(shape):
    nd = len(shape)
    return pl.BlockSpec(shape, lambda i: (0,) * nd, pipeline_mode=pl.Buffered(1))


def _layer_spec(stacked_shape, layer, cols=None):
    _, rows, width = stacked_shape
    return pl.BlockSpec((None, rows, cols or width), lambda i: (layer, 0, 0), pipeline_mode=pl.Buffered(1))


def _pipeline_specs(tb, width, nb, stages):
    return [pl.BlockSpec((tb, width), functools.partial(lambda i, s: (jnp.clip(i - s, 0, nb - 1), 0), s=s))
            for s in range(stages)]


def _params():
    return pltpu.CompilerParams(dimension_semantics=("arbitrary",), vmem_limit_bytes=VMEM_LIMIT_BYTES)


FFN_PIECE = 256


class _FfnPieces:
    def __init__(self, x, wgu_ref, wd_ref):
        self.x = x
        self.xb = x.astype(BF16)
        self.wgu_ref = wgu_ref
        self.wd_ref = wd_ref
        self.n_up = FFN_HIDDEN // FFN_PIECE
        self.n_down = D_MODEL // FFN_PIECE
        self.done = 0
        self.act = []
        self.out = []

    def advance_to(self, fraction):
        total = self.n_up + self.n_down
        target = min(total, int(fraction * total + 1e-9))
        while self.done < target:
            if self.done < self.n_up:
                lo = self.done * FFN_PIECE
                gate = _dot(self.xb, self.wgu_ref[:, lo:lo + FFN_PIECE])
                up = _dot(self.xb, self.wgu_ref[:, FFN_HIDDEN + lo:FFN_HIDDEN + lo + FFN_PIECE])
                self.act.append((gate * _sigmoid(gate) * up).astype(BF16))
            else:
                if len(self.act) > 1:
                    self.act = [jnp.concatenate(self.act, axis=1)]
                lo = (self.done - self.n_up) * FFN_PIECE
                self.out.append(_dot(self.act[0], self.wd_ref[:, lo:lo + FFN_PIECE]))
            self.done += 1

    def finish(self, g_ref, b_ref):
        self.advance_to(1.0)
        return _layernorm(ALPHA * self.x + jnp.concatenate(self.out, axis=1), g_ref[...], b_ref[...])


def _hgrn_kernel(xf_ref, xb_ref, win_ref, lb_ref, nw_ref, wout_ref, g_ref, b_ref, tri_ref,
                 wgu_ref, wd_ref, g2_ref, b2_ref, o_ref,
                 q_s, k_s, v_s, b_s, gate_s, o_s, state_s, x1_s, *, tb):
    nc = tb // CHUNK

    @pl.when(pl.program_id(0) == 0)
    def _():
        state_s[...] = jnp.zeros_like(state_s)
        for ref in (q_s, k_s, v_s, b_s, gate_s, x1_s):
            ref[...] = jnp.zeros_like(ref)

    ffn = _FfnPieces(x1_s[...], wgu_ref, wd_ref)

    proj = _dot(xf_ref[...].astype(BF16), win_ref[...])
    q = proj[:, 0:D_MODEL]
    f = proj[:, D_MODEL:2 * D_MODEL]
    v = proj[:, 2 * D_MODEL:3 * D_MODEL]
    gt = proj[:, 3 * D_MODEL:4 * D_MODEL]
    lb = lb_ref[...]
    a1 = jnp.log(lb)
    a2 = jnp.log1p(-lb) + _log_sigmoid(f)
    log_f = jnp.maximum(a1, a2) + _log1p_exp_neg(jnp.abs(a1 - a2))
    k = (1.0 - lb) * _sigmoid(-f)
    q = q * _sigmoid(q)
    gate = gt * _sigmoid(gt)
    bcum = _chunk_cumsum(tri_ref[...], log_f) * LOG2E

    row = lax.broadcasted_iota(jnp.int32, (CHUNK, LANES), 0)
    rowi = lax.broadcasted_iota(jnp.int32, (CHUNK, CHUNK), 0)
    coli = lax.broadcasted_iota(jnp.int32, (CHUNK, CHUNK), 1)
    row8 = lax.broadcasted_iota(jnp.int32, (SUB, CHUNK), 0)
    lane8 = lax.broadcasted_iota(jnp.int32, (SUB, CHUNK), 1)
    place = [[((lane8 == blk * SUB + s) & (row8 >= s)).astype(F32) for s in range(SUB)]
             for blk in range(CHUNK // SUB)]
    same_block = {}
    m = SUB
    while m < CHUNK:
        shift = int(math.log2(2 * m))
        same_block[m] = ((rowi >> shift) == (coli >> shift)).astype(F32)
        m *= 2
    nw = nw_ref[...]

    def head_scores(h, rows):
        qc = q_s[h, rows, :]
        kc = k_s[h, rows, :]
        bc = b_s[h, rows, :]
        att = jnp.zeros((CHUNK, CHUNK), F32)
        m = SUB
        while m < CHUNK:
            ref = jnp.concatenate(
                [jnp.broadcast_to(bc[blk * 2 * m + m - 1:blk * 2 * m + m, :], (2 * m, LANES))
                 for blk in range(CHUNK // (2 * m))], axis=0)
            upper = (row & m) != 0
            qh = qc * jnp.exp2(jnp.where(upper, bc - ref, NEG_INF))
            kh = kc * jnp.exp2(jnp.where(upper, NEG_INF, ref - bc))
            pair = _dot_nt(qh.astype(BF16), kh.astype(BF16))
            att = att + pair * same_block[m]
            m *= 2
        tiles = []
        for blk in range(CHUNK // SUB):
            sl = slice(blk * SUB, (blk + 1) * SUB)
            qi, bi = qc[sl], bc[sl]
            tile = jnp.zeros((SUB, CHUNK), F32)
            for s in range(SUB):
                r = rows.start + blk * SUB + s
                ks = jnp.broadcast_to(k_s[h, r:r + 1, :], (SUB, LANES))
                bs = jnp.broadcast_to(b_s[h, r:r + 1, :], (SUB, LANES))
                decay = jnp.exp2(jnp.minimum(bi - bs, 0.0))
                col = jnp.sum(qi * (ks * decay), axis=1, keepdims=True)
                tile = tile + col * place[blk][s]
            tiles.append(tile)
        return (att + jnp.concatenate(tiles, axis=0)).astype(BF16)

    def head_output(h, rows, att):
        st = state_s[h]
        qc = q_s[h, rows, :]
        kc = k_s[h, rows, :]
        vcb = v_s[h, rows, :].astype(BF16)
        bc = b_s[h, rows, :]
        b_last = bc[CHUNK - 1:CHUNK, :]
        o_s[h, rows, :] = _dot(att, vcb) + _dot_nt((qc * jnp.exp2(bc)).astype(BF16), st.astype(BF16))
        k_dec = kc * jnp.exp2(b_last - bc)
        state_s[h] = st * jnp.exp2(b_last) + _dot_tn(vcb, k_dec.astype(BF16))

    n_units = 2 * nc * HGRN_HEADS
    unit = 0
    for c in range(nc):
        rows = slice(c * CHUNK, (c + 1) * CHUNK)
        atts = []
        for h in range(HGRN_HEADS):
            atts.append(head_scores(h, rows))
            unit += 1
            ffn.advance_to(unit / n_units)
        for h in range(HGRN_HEADS):
            head_output(h, rows, atts[h])
            unit += 1
            ffn.advance_to(unit / n_units)
    o_ref[...] = ffn.finish(g2_ref, b2_ref)

    normed = []
    for h in range(HGRN_HEADS):
        o = o_s[h]
        ms = jnp.mean(o * o, axis=-1, keepdims=True)
        normed.append((o * lax.rsqrt(ms + HEAD_NORM_EPS) * nw * gate_s[h]).astype(BF16))
    y = _dot(jnp.concatenate(normed, axis=1), wout_ref[...])
    x1_new = _layernorm(ALPHA * xb_ref[...] + y, g_ref[...], b_ref[...])

    x1_s[...] = x1_new
    for h in range(HGRN_HEADS):
        sl = slice(h * HGRN_DK, (h + 1) * HGRN_DK)
        q_s[h] = q[:, sl]
        k_s[h] = k[:, sl]
        v_s[h] = v[:, sl]
        b_s[h] = bcum[:, sl]
        gate_s[h] = gate[:, sl]


def _chunk_tri(tb):
    i = jnp.arange(tb)
    same = (i[:, None] // CHUNK) == (i[None, :] // CHUNK)
    return (same & (i[None, :] <= i[:, None])).astype(BF16)


def _hgrn_layer(x, w_in, norm_w, w_out, layer, lb, g, b, w_gate_up, w_down, ffn_layer, g2, b2, tb):
    seq = x.shape[0]
    nb = seq // tb
    head_scratch = pltpu.VMEM((HGRN_HEADS, tb, HGRN_DK), F32)
    spec0, spec1, spec2 = _pipeline_specs(tb, D_MODEL, nb, 3)
    vec = _const_spec((1, D_MODEL))
    return pl.pallas_call(
        functools.partial(_hgrn_kernel, tb=tb),
        out_shape=jax.ShapeDtypeStruct((seq, D_MODEL), F32),
        grid=(nb + 2,),
        in_specs=[spec0, spec1, _layer_spec(w_in.shape, layer), vec, _const_spec((1, HGRN_DV)),
                  _layer_spec(w_out.shape, layer), vec, vec, _const_spec((tb, tb)),
                  _layer_spec(w_gate_up.shape, ffn_layer), _layer_spec(w_down.shape, ffn_layer), vec, vec],
        out_specs=spec2,
        scratch_shapes=[head_scratch] * 6 + [pltpu.VMEM((HGRN_HEADS, HGRN_DV, HGRN_DK), F32),
                                             pltpu.VMEM((tb, D_MODEL), F32)],
        compiler_params=_params(),
        name="hgrn_layer",
    )(x, x, w_in, lb, norm_w, w_out, g, b, _chunk_tri(tb), w_gate_up, w_down, g2, b2)


def _mlstm_kernel(xf_ref, xb_ref, win_ref, wg_ref, conv_ref, gb_ref, nw_ref, wout_ref, g_ref, b_ref, tri_ref, sel_ref,
                  wgu_ref, wd_ref, g2_ref, b2_ref, o_ref,
                  pad_s, q_s, k_s, v_s, crow_s, gcol_s, wi_s, em_s, ws_s, wp_s, sigo_s, h_s, c_s, m_s, x1_s, *, tb):
    nc = tb // CHUNK
    halo = SUB

    @pl.when(pl.program_id(0) == 0)
    def _():
        pad_s[0:halo, :] = jnp.zeros((halo, 2 * MLSTM_QK_W), F32)
        c_s[...] = jnp.zeros_like(c_s)
        m_s[...] = jnp.zeros_like(m_s)
        for ref in (q_s, k_s, v_s, crow_s, gcol_s, wi_s, ws_s, wp_s, sigo_s, x1_s):
            ref[...] = jnp.zeros_like(ref)
        em_s[...] = jnp.ones_like(em_s)

    ffn = _FfnPieces(x1_s[...], wgu_ref, wd_ref)
    xb = xf_ref[...].astype(BF16)
    proj = _dot(xb, win_ref[...])
    v_new = proj[:, 2 * MLSTM_QK_W:2 * MLSTM_QK_W + D_MODEL]
    o_pre = proj[:, 2 * MLSTM_QK_W + D_MODEL:]
    pad_s[halo:halo + tb, :] = proj[:, :2 * MLSTM_QK_W]
    conv_w = conv_ref[...]
    qk = pad_s[pl.ds(halo - MLSTM_CONV + 1, tb), :] * conv_w[0:1, :]
    for j in range(1, MLSTM_CONV):
        qk = qk + pad_s[pl.ds(halo - MLSTM_CONV + 1 + j, tb), :] * conv_w[j:j + 1, :]
    pad_s[0:halo, :] = pad_s[tb:tb + halo, :]
    qk = qk * _sigmoid(qk)
    q_new = qk[:, :MLSTM_QK_W]
    k_new = qk[:, MLSTM_QK_W:] * (MLSTM_DQK ** -0.5)
    gates = _dot(xb, wg_ref[...]) + gb_ref[...]
    li = gates[:, :LANES]
    bcum = _chunk_cumsum(tri_ref[...], _log_sigmoid(gates[:, LANES:]))
    cq = li - bcum
    rin = lax.broadcasted_iota(jnp.int32, (tb, LANES), 0) & (CHUNK - 1)
    cm = cq
    k = 1
    while k < CHUNK:
        shifted = pltpu.roll(cm, k, 0)
        cm = jnp.maximum(cm, jnp.where(rin >= k, shifted, NEG_INF))
        k *= 2
    m_prev = m_s[...]
    g_parts, wi_parts, em_parts, ws_parts, wp_parts, crow_new = [], [], [], [], [], []
    for c in range(nc):
        sl = slice(c * CHUNK, (c + 1) * CHUNK)
        mx = jnp.maximum(m_prev, cm[(c + 1) * CHUNK - 1:(c + 1) * CHUNK])
        g_c = jnp.maximum(cm[sl], m_prev)
        g_parts.append(g_c)
        wi_parts.append(jnp.exp(m_prev - g_c))
        em_parts.append(jnp.minimum(jnp.exp(-(bcum[sl] + g_c)), BF16_MAX))
        ws_parts.append(jnp.exp(cq[sl] - mx))
        wp_parts.append(jnp.exp(m_prev - mx))
        m_prev = bcum[(c + 1) * CHUNK - 1:(c + 1) * CHUNK] + mx
        crow_new.append(cq[sl].T[0:SUB, :])
    m_s[...] = m_prev
    wp_parts.append(jnp.zeros((SUB - nc, LANES), F32))
    def spread_lanes(tile, terms):
        parts, rest = [], tile
        for _ in range(terms):
            part = rest.astype(BF16)
            parts.append(part)
            rest = rest - part.astype(F32)
        return _dot(jnp.concatenate(parts, axis=1), sel_ref[0:terms * LANES, :])

    g_spread = spread_lanes(jnp.concatenate(g_parts, axis=0), 3)
    w_spread = spread_lanes(jnp.concatenate(wi_parts + em_parts + ws_parts + wp_parts, axis=0), 2)
    sig_o = _sigmoid(o_pre)

    lane_c = lax.broadcasted_iota(jnp.int32, (CHUNK, LANES), 1)
    rowi = lax.broadcasted_iota(jnp.int32, (CHUNK, CHUNK), 0)
    coli = lax.broadcasted_iota(jnp.int32, (CHUNK, CHUNK), 1)
    causal = rowi >= coli
    first_head = lane_c < MLSTM_DQK
    first_rows = lax.broadcasted_iota(jnp.int32, (2 * MLSTM_DQK, 2 * MLSTM_DV), 0) < MLSTM_DQK
    ones_blk = jnp.ones((CHUNK, LANES), BF16)
    nw_all = nw_ref[...]

    n_pairs = MLSTM_HEADS // 2
    n_units = 2 * nc * n_pairs
    unit = 0
    for c in range(nc):
        rows = slice(c * CHUNK, (c + 1) * CHUNK)
        crow = crow_s[c]
        wp_row = wp_s[c:c + 1, :]
        staged = []
        for p in range(n_pairs):
            lanes_p = slice(p * LANES, (p + 1) * LANES)
            cext = c_s[p]
            q_pair = q_s[rows, lanes_p]
            k_pair = k_s[rows, lanes_p]
            hsl = [slice((2 * p + j) * MLSTM_DV, (2 * p + j + 1) * MLSTM_DV) for j in range(2)]
            qm = jnp.concatenate([jnp.where(first_head, q_pair, 0.0), jnp.where(first_head, 0.0, q_pair)],
                                 axis=0).astype(BF16)
            v_ext = [jnp.concatenate([v_s[rows, hsl[j]].astype(BF16), ones_blk], axis=1) for j in range(2)]
            scores = _dot_nt(qm, k_pair.astype(BF16))
            inter = _dot(qm, cext.astype(BF16))
            kw = jnp.concatenate([jnp.where(first_head, k_pair * ws_s[rows, hsl[0]], 0.0),
                                  jnp.where(first_head, 0.0, k_pair * ws_s[rows, hsl[1]])], axis=0).astype(BF16)
            update = _dot_tn(kw, jnp.concatenate(v_ext, axis=0))
            w_prev = jnp.where(first_rows, jnp.tile(wp_row[:, hsl[0]], (1, 2)), jnp.tile(wp_row[:, hsl[1]], (1, 2)))
            c_s[p] = w_prev * cext + update
            staged.append((hsl, v_ext, scores, inter))
            unit += 1
            ffn.advance_to(unit / n_units)
        for p in range(n_pairs):
            hsl, v_ext, scores, inter = staged[p]
            for j in range(2):
                h = 2 * p + j
                hr = slice(j * CHUNK, (j + 1) * CHUNK)
                d_mat = jnp.exp(jnp.where(causal, crow[h:h + 1, :] - gcol_s[rows, h * MLSTM_DV:h * MLSTM_DV + CHUNK],
                                          NEG_INF))
                s = scores[hr] * d_mat
                both = _dot(s.astype(BF16), v_ext[j]) + jnp.tile(wi_s[rows, hsl[j]], (1, 2)) * inter[hr]
                floor = em_s[rows, hsl[j]]
                hh = both[:, :MLSTM_DV] / jnp.maximum(jnp.abs(both[:, MLSTM_DV:]), floor)
                h_s[rows, hsl[j]] = jnp.where(floor >= BF16_MAX, 0.0, hh)
            unit += 1
            ffn.advance_to(unit / n_units)
    o_ref[...] = ffn.finish(g2_ref, b2_ref)

    normed = []
    for h in range(MLSTM_HEADS):
        hsl = slice(h * MLSTM_DV, (h + 1) * MLSTM_DV)
        hh = h_s[:, hsl]
        ms = jnp.mean(hh * hh, axis=-1, keepdims=True)
        normed.append((hh * lax.rsqrt(ms + HEAD_NORM_EPS) * nw_all[:, hsl] * sigo_s[:, hsl]).astype(BF16))
    y = _dot(jnp.concatenate(normed, axis=1), wout_ref[...])
    x1_new = _layernorm(ALPHA * xb_ref[...] + y, g_ref[...], b_ref[...])

    x1_s[...] = x1_new
    q_s[...] = q_new
    k_s[...] = k_new
    v_s[...] = v_new
    sigo_s[...] = sig_o
    for c in range(nc):
        crow_s[c] = crow_new[c]
    gcol_s[...] = g_spread
    wi_s[...] = w_spread[0:tb]
    em_s[...] = w_spread[tb:2 * tb]
    ws_s[...] = w_spread[2 * tb:3 * tb]
    wp_s[...] = w_spread[3 * tb:3 * tb + SUB]


MLSTM_MAIN_W = 2 * MLSTM_QK_W + 2 * D_MODEL


def _mlstm_gate_weights(w_in, gate_b):
    pad = LANES - MLSTM_HEADS
    w_i = jnp.pad(w_in[:, MLSTM_MAIN_W:MLSTM_MAIN_W + MLSTM_HEADS], ((0, 0), (0, pad)))
    w_f = jnp.pad(w_in[:, MLSTM_MAIN_W + MLSTM_HEADS:], ((0, 0), (0, pad)))
    b_i = jnp.pad(gate_b[:MLSTM_HEADS], (0, pad))
    b_f = jnp.pad(gate_b[MLSTM_HEADS:], (0, pad))
    return jnp.concatenate([w_i, w_f], axis=1).astype(BF16), jnp.concatenate([b_i, b_f]).reshape(1, 2 * LANES)


def _lane_spread_matrix():
    src = jnp.arange(LANES)[:, None]
    dst = jnp.arange(D_MODEL)[None, :] // LANES
    return jnp.tile((src == dst).astype(BF16), (3, 1))


def _mlstm_layer(x, w_in, w_gate, conv_w, gate_b, norm_w, w_out, layer, g, b,
                 w_gate_up, w_down, ffn_layer, g2, b2, tb):
    seq = x.shape[0]
    nb = seq // tb
    assert tb // CHUNK <= SUB
    wide = pltpu.VMEM((tb, D_MODEL), F32)
    sel = _lane_spread_matrix()
    spec0, spec1, spec2 = _pipeline_specs(tb, D_MODEL, nb, 3)
    vec = _const_spec((1, D_MODEL))
    return pl.pallas_call(
        functools.partial(_mlstm_kernel, tb=tb),
        out_shape=jax.ShapeDtypeStruct((seq, D_MODEL), F32),
        grid=(nb + 2,),
        in_specs=[spec0, spec1, _layer_spec(w_in.shape, layer, MLSTM_MAIN_W), _const_spec(w_gate.shape),
                  _const_spec(conv_w.shape), _const_spec((1, 2 * LANES)), vec,
                  _layer_spec(w_out.shape, layer), vec, vec, _const_spec((tb, tb)), _const_spec(sel.shape),
                  _layer_spec(w_gate_up.shape, ffn_layer), _layer_spec(w_down.shape, ffn_layer), vec, vec],
        out_specs=spec2,
        scratch_shapes=[pltpu.VMEM((tb + SUB, 2 * MLSTM_QK_W), F32),
                        pltpu.VMEM((tb, MLSTM_QK_W), F32), pltpu.VMEM((tb, MLSTM_QK_W), F32),
                        wide, pltpu.VMEM((tb // CHUNK, SUB, CHUNK), F32), wide, wide, wide, wide,
                        pltpu.VMEM((SUB, D_MODEL), F32), wide, wide,
                        pltpu.VMEM((MLSTM_HEADS // 2, 2 * MLSTM_DQK, 2 * MLSTM_DV), F32),
                        pltpu.VMEM((1, LANES), F32), wide],
        compiler_params=_params(),
        name="mlstm_layer",
    )(x, x, w_in, w_gate, conv_w, gate_b, norm_w, w_out, g, b, _chunk_tri(tb), sel, w_gate_up, w_down, g2, b2)


S5_LANE_GROUPS = LANES // S5_GROUP_CH
S5_BLOCKS = D_MODEL // LANES
S5_BLOCK_STATE = S5_LANE_GROUPS * S5_STATE
S5_SCAN_WIDTH = 128
S5_TAPS = 4


def _s5_kernel(x_ref, win_ref, bre_ref, bim_ref, cre_ref, cim_ref, are_ref, aim_ref, d_ref, wout_ref,
               g_ref, b_ref, wgu_ref, wd_ref, g2_ref, b2_ref, o_ref, hr_s, hi_s, x1_s, *, tb):
    @pl.when(pl.program_id(0) == 0)
    def _():
        hr_s[...] = jnp.zeros_like(hr_s)
        hi_s[...] = jnp.zeros_like(hi_s)
        x1_s[...] = jnp.zeros_like(x1_s)

    ffn = _FfnPieces(x1_s[...], wgu_ref, wd_ref)
    x = x_ref[...]
    u = _dot(x.astype(BF16), win_ref[...])
    ub = u.astype(BF16)
    nt = tb // SUB
    row8 = lax.broadcasted_iota(jnp.int32, (SUB, S5_SCAN_WIDTH), 0)

    def tile_scan(zr, zi, mults, first=0):
        for i in range(first, len(mults)):
            mr, mi = mults[i]
            rr = pltpu.roll(zr, 1 << i, 1)
            ri = pltpu.roll(zi, 1 << i, 1)
            zr, zi = zr + (mr * rr - mi * ri), zi + (mr * ri + mi * rr)
        return zr, zi

    row_u = lax.broadcasted_iota(jnp.int32, (tb, D_MODEL), 0) & (SUB - 1)
    u_taps = [ub] + [jnp.where(row_u >= k, pltpu.roll(u, k, 0), 0.0).astype(BF16) for k in range(1, S5_TAPS)]

    def input_states(j, cols):
        lanes = slice(j * LANES, (j + 1) * LANES)
        uj = jnp.concatenate([t[:, lanes] for t in u_taps], axis=1)
        return (_dot(uj, bre_ref[j, :, cols]).reshape(nt, SUB, S5_SCAN_WIDTH),
                _dot(uj, bim_ref[j, :, cols]).reshape(nt, SUB, S5_SCAN_WIDTH))

    n_slabs = S5_BLOCK_STATE // S5_SCAN_WIDTH
    y_blocks = []
    for j, slab in [(j, slab) for j in range(S5_BLOCKS) for slab in range(n_slabs)]:
        cols = slice(slab * S5_SCAN_WIDTH, (slab + 1) * S5_SCAN_WIDTH)
        st = slice(j * S5_BLOCK_STATE + cols.start, j * S5_BLOCK_STATE + cols.stop)
        ar = are_ref[:, st]
        ai = aim_ref[:, st]
        mults = []
        pr, pi = ar, ai
        for i in range(3):
            keep = row8 >= (1 << i)
            mults.append((jnp.where(keep, pr, 0.0), jnp.where(keep, pi, 0.0)))
            pr, pi = pr * pr - pi * pi, 2.0 * pr * pi
        pwr, pwi = tile_scan(jnp.where(row8 == 0, ar, 0.0)[None], jnp.where(row8 == 0, ai, 0.0)[None], mults)
        pwr, pwi = pwr[0], pwi[0]
        sr, si = input_states(j, cols)
        ffn.advance_to((j * n_slabs + slab + 1) / (S5_BLOCKS * n_slabs))
        sr, si = tile_scan(sr, si, mults, first=int(math.log2(S5_TAPS)))
        cr = hr_s[:, st]
        ci = hi_s[:, st]
        tiles_r, tiles_i = [], []
        for n in range(nt):
            tr = sr[n] + (pwr * cr - pwi * ci)
            ti = si[n] + (pwr * ci + pwi * cr)
            tiles_r.append(tr)
            tiles_i.append(ti)
            cr, ci = tr[SUB - 1:SUB, :], ti[SUB - 1:SUB, :]
        hr_s[:, st] = cr
        hi_s[:, st] = ci
        hr = jnp.concatenate(tiles_r, axis=0)
        hi = jnp.concatenate(tiles_i, axis=0)
        y_slab = _dot(hr.astype(BF16), cre_ref[j, cols, :]) - _dot(hi.astype(BF16), cim_ref[j, cols, :])
        if slab == 0:
            y_blocks.append(y_slab)
        else:
            y_blocks[j] = y_blocks[j] + y_slab
    o_ref[...] = ffn.finish(g2_ref, b2_ref)

    y = jax.nn.gelu(jnp.concatenate(y_blocks, axis=1) + d_ref[...] * u)
    z = _dot(y.astype(BF16), wout_ref[...])
    mix = z[:, :D_MODEL] * _sigmoid(z[:, D_MODEL:])
    x1_s[...] = _layernorm(ALPHA * x + mix, g_ref[...], b_ref[...])


def _s5_block_diag(t):
    a, b = t.shape[1], t.shape[2]
    t = t.reshape(S5_BLOCKS, S5_LANE_GROUPS, a, b)
    eye = jnp.eye(S5_LANE_GROUPS, dtype=t.dtype)
    out = jnp.einsum("jgab,gh->jgahb", t, eye)
    return out.reshape(S5_BLOCKS, S5_LANE_GROUPS * a, S5_LANE_GROUPS * b)


def _s5_layer(x, w_in, a_re, a_im, log_dt, b_re, b_im, c_re, c_im, d_skip, w_out, layer, g, b,
              w_gate_up, w_down, ffn_layer, g2, b2, tb):
    seq = x.shape[0]
    nb = seq // tb
    dt = jnp.exp(log_dt)[:, None]
    mag = jnp.exp(a_re * dt)
    abar_re = mag * jnp.cos(a_im * dt)
    abar_im = mag * jnp.sin(a_im * dt)
    nr = abar_re - 1.0
    ni = abar_im
    den = a_re * a_re + a_im * a_im
    coef_re = (nr * a_re + ni * a_im) / den
    coef_im = (ni * a_re - nr * a_im) / den
    bbar_re = coef_re[..., None] * b_re - coef_im[..., None] * b_im
    bbar_im = coef_re[..., None] * b_im + coef_im[..., None] * b_re
    taps_re, taps_im = [bbar_re], [bbar_im]
    for _ in range(1, S5_TAPS):
        pr, pi = taps_re[-1], taps_im[-1]
        taps_re.append(abar_re[..., None] * pr - abar_im[..., None] * pi)
        taps_im.append(abar_re[..., None] * pi + abar_im[..., None] * pr)
    stack_taps = lambda taps: jnp.concatenate(
        [_s5_block_diag(jnp.swapaxes(t, 1, 2)) for t in taps], axis=1).astype(BF16)
    bre = stack_taps(taps_re)
    bim = stack_taps(taps_im)
    cre = _s5_block_diag(jnp.swapaxes(c_re, 1, 2)).astype(BF16)
    cim = _s5_block_diag(jnp.swapaxes(c_im, 1, 2)).astype(BF16)
    n_state = S5_GROUPS * S5_STATE
    spec0, spec1 = _pipeline_specs(tb, D_MODEL, nb, 2)
    vec = _const_spec((1, D_MODEL))
    return pl.pallas_call(
        functools.partial(_s5_kernel, tb=tb),
        out_shape=jax.ShapeDtypeStruct((seq, D_MODEL), F32),
        grid=(nb + 1,),
        in_specs=[spec0, _layer_spec(w_in.shape, layer), _const_spec(bre.shape),
                  _const_spec(bim.shape), _const_spec(cre.shape), _const_spec(cim.shape), _const_spec((1, n_state)),
                  _const_spec((1, n_state)), vec, _layer_spec(w_out.shape, layer), vec, vec,
                  _layer_spec(w_gate_up.shape, ffn_layer), _layer_spec(w_down.shape, ffn_layer), vec, vec],
        out_specs=spec1,
        scratch_shapes=[pltpu.VMEM((1, n_state), F32), pltpu.VMEM((1, n_state), F32),
                        pltpu.VMEM((tb, D_MODEL), F32)],
        compiler_params=_params(),
        name="s5_layer",
    )(x, w_in, bre, bim, cre, cim, abar_re.reshape(1, n_state), abar_im.reshape(1, n_state),
      d_skip.reshape(1, D_MODEL), w_out, g, b, w_gate_up, w_down, g2, b2)


def _hgrn_lower_bounds(lb_logits):
    p = jax.nn.softmax(lb_logits.astype(F32), axis=0)
    c = jnp.cumsum(p, axis=0)
    return c - c[0:1]


def kernel(x, hgrn_w_in, hgrn_norm_w, hgrn_w_out, hgrn_lb_logits, mlstm_w_in, mlstm_conv_w, mlstm_gate_b, mlstm_norm_w, mlstm_w_out, s5_w_in, s5_a_re, s5_a_im, s5_log_dt, s5_b_re, s5_b_im, s5_c_re, s5_c_im, s5_d, s5_w_out, ffn_w_gate_up, ffn_w_down, ln_g, ln_b):
    bsz, seq, _ = x.shape
    tb = math.gcd(seq, ROW_BLOCK)
    assert tb % CHUNK == 0, "sequence length must be a multiple of the recurrence chunk"
    lb_all = _hgrn_lower_bounds(hgrn_lb_logits)
    row = lambda t: t.reshape(1, -1)
    (hgrn_w_in, hgrn_w_out, mlstm_w_in_b, mlstm_w_out, s5_w_in, s5_w_out, ffn_w_gate_up, ffn_w_down) = (
        w.astype(BF16) for w in (hgrn_w_in, hgrn_w_out, mlstm_w_in, mlstm_w_out, s5_w_in, s5_w_out,
                                 ffn_w_gate_up, ffn_w_down))
    outs = []
    for bi in range(bsz):
        h = x[bi]
        for i in range(DEPTH):
            kind = i % N_MIXERS
            j = i // N_MIXERS
            g0, b0 = row(ln_g[i, 0]), row(ln_b[i, 0])
            ffn_args = (ffn_w_gate_up, ffn_w_down, i, row(ln_g[i, 1]), row(ln_b[i, 1]), tb)
            if kind == 0:
                h = _hgrn_layer(h, hgrn_w_in, row(hgrn_norm_w[j]), hgrn_w_out, j, row(lb_all[i]), g0, b0, *ffn_args)
            elif kind == 1:
                w_gate, gate_b = _mlstm_gate_weights(mlstm_w_in[j], mlstm_gate_b[j])
                h = _mlstm_layer(h, mlstm_w_in_b, w_gate, mlstm_conv_w[j], gate_b, row(mlstm_norm_w[j]),
                                 mlstm_w_out, j, g0, b0, *ffn_args)
            else:
                h = _s5_layer(h, s5_w_in, s5_a_re[j], s5_a_im[j], s5_log_dt[j], s5_b_re[j], s5_b_im[j],
                              s5_c_re[j], s5_c_im[j], s5_d[j], s5_w_out, j, g0, b0, *ffn_args)
        outs.append(h)
    return jnp.stack(outs, axis=0)
```
